```python
import math
import jax, jax.numpy as jnp
from jax import lax
import numpy as np

D_MODEL = 4096
BATCH = 2
SEQ = 4096
DEPTH = 4

N_MIXERS = 3
HEAD_DIM = 128
N_HEADS = D_MODEL // HEAD_DIM
DIFF_HEADS = D_MODEL // (2 * HEAD_DIM)
ROT_DIM = HEAD_DIM // 4
ROPE_THETA = 500000.0
D_FF = D_MODEL * 11 // 8
RMS_EPS = 1e-6
SUBLN_EPS = 1e-5
ATTN_Q_BLOCK = 128
MOBA_BLOCK = 256
MOBA_TOPK = 3
MOBA_Q_CHUNK = 16
DSA_TOPK_MAX = 256
DSA_Q_CHUNK = 64
IDX_HEADS = D_MODEL // 128
IDX_DIM = 128
N_LAYERS_A = (DEPTH + 2) // 3
N_LAYERS_B = (DEPTH + 1) // 3
N_LAYERS_C = DEPTH // 3

kernel_name = "hybrid_diff_moba_dsa_macaron"


def rms_norm(x, g, eps=RMS_EPS):
    xf = x.astype(jnp.float32)
    y = xf * lax.rsqrt(jnp.mean(xf * xf, axis=-1, keepdims=True) + eps) * g.astype(jnp.float32)
    return y.astype(x.dtype)


def swiglu(x, w_gate, w_up, w_down):
    return (jax.nn.silu(x @ w_gate) * (x @ w_up)) @ w_down


def rope_tables(positions):
    inv = ROPE_THETA ** (-jnp.arange(0, ROT_DIM, 2, dtype=jnp.float32) / ROT_DIM)
    ang = positions.astype(jnp.float32)[..., None] * inv
    return jnp.cos(ang)[:, :, None, :], jnp.sin(ang)[:, :, None, :]


def apply_rope(x, cos, sin):
    half = ROT_DIM // 2
    c = cos.astype(x.dtype)
    s = sin.astype(x.dtype)
    x1 = x[..., :half]
    x2 = x[..., half:ROT_DIM]
    return jnp.concatenate([x1 * c - x2 * s, x2 * c + x1 * s, x[..., ROT_DIM:]], axis=-1)


def lambda_init_for(layer):
    return 0.8 - 0.6 * math.exp(-0.3 * layer)


def diff_attention(qkv, cos, sin, lq1, lk1, lq2, lk2, subln_g, lam_init):
    B, S, _ = qkv.shape
    q, k, v = jnp.split(qkv, 3, axis=-1)
    q = apply_rope(q.reshape(B, S, 2 * DIFF_HEADS, HEAD_DIM), cos, sin).reshape(B, S, DIFF_HEADS, 2, HEAD_DIM)
    k = apply_rope(k.reshape(B, S, 2 * DIFF_HEADS, HEAD_DIM), cos, sin).reshape(B, S, DIFF_HEADS, 2, HEAD_DIM)
    qh = q.transpose(0, 2, 3, 1, 4)
    kh = k.transpose(0, 2, 3, 1, 4)
    vh = v.reshape(B, S, DIFF_HEADS, 2 * HEAD_DIM).transpose(0, 2, 1, 3)
    f32 = jnp.float32
    lam = (jnp.exp(jnp.sum(lq1.astype(f32) * lk1.astype(f32)))
           - jnp.exp(jnp.sum(lq2.astype(f32) * lk2.astype(f32))) + lam_init)
    scale = HEAD_DIM ** -0.5
    kpos = jnp.arange(S)

    def block(c):
        q0 = c * ATTN_Q_BLOCK
        qc = lax.dynamic_slice_in_dim(qh, q0, ATTN_Q_BLOCK, axis=3)
        s = jnp.einsum('bhcqd,bhckd->bhcqk', qc, kh).astype(f32) * scale
        qpos = q0 + jnp.arange(ATTN_Q_BLOCK)
        s = jnp.where(kpos[None, :] <= qpos[:, None], s, -jnp.inf)
        p = jax.nn.softmax(s, axis=-1)
        a = p[:, :, 0] - lam * p[:, :, 1]
        return jnp.einsum('bhqk,bhkd->bhqd', a.astype(vh.dtype), vh)

    o = lax.map(block, jnp.arange(S // ATTN_Q_BLOCK))
    o = o.transpose(1, 0, 3, 2, 4).reshape(B, S, DIFF_HEADS, 2 * HEAD_DIM)
    o = rms_norm(o, subln_g, SUBLN_EPS) * (1.0 - lam_init)
    return o.reshape(B, S, D_MODEL)


def moba_attention(qkv, cos, sin):
    B, S, _ = qkv.shape
    q, k, v = jnp.split(qkv, 3, axis=-1)
    q = apply_rope(q.reshape(B, S, N_HEADS, HEAD_DIM), cos, sin).transpose(0, 2, 1, 3)
    k = apply_rope(k.reshape(B, S, N_HEADS, HEAD_DIM), cos, sin).transpose(0, 2, 1, 3)
    v = v.reshape(B, S, N_HEADS, HEAD_DIM).transpose(0, 2, 1, 3)
    nb = -(-S // MOBA_BLOCK)
    pad = nb * MOBA_BLOCK - S
    kp = jnp.pad(k, ((0, 0), (0, 0), (0, pad), (0, 0)))
    vp = jnp.pad(v, ((0, 0), (0, 0), (0, pad), (0, 0)))
    kb = kp.reshape(B, N_HEADS, nb, MOBA_BLOCK, HEAD_DIM)
    vb = vp.reshape(B, N_HEADS, nb, MOBA_BLOCK, HEAD_DIM)
    k_mean = jnp.mean(kb.astype(jnp.float32), axis=3).astype(k.dtype)
    qpos = jnp.arange(S)
    gate = jnp.einsum('bhsd,bhnd->bhsn', q, k_mean).astype(jnp.float32)
    gate = jnp.where(jnp.arange(nb)[None, :] < (qpos // MOBA_BLOCK)[:, None], gate, -jnp.inf)
    n_sel = min(MOBA_TOPK, nb)
    _, sel = lax.top_k(gate, n_sel)
    scale = HEAD_DIM ** -0.5
    bi = jnp.arange(B)[:, None, None, None]
    hi = jnp.arange(N_HEADS)[None, :, None, None]

    def chunk(c):
        q0 = c * MOBA_Q_CHUNK
        qc = lax.dynamic_slice_in_dim(q, q0, MOBA_Q_CHUNK, axis=2)
        sc = lax.dynamic_slice_in_dim(sel, q0, MOBA_Q_CHUNK, axis=2)
        cpos = q0 + jnp.arange(MOBA_Q_CHUNK)
        blk = q0 // MOBA_BLOCK
        kg = kb[bi, hi, sc]
        vg = vb[bi, hi, sc]
        s_sel = jnp.einsum('bhqd,bhqjkd->bhqjk', qc, kg).astype(jnp.float32) * scale
        valid = jnp.arange(n_sel)[None, :] < (cpos // MOBA_BLOCK)[:, None]
        s_sel = jnp.where(valid[:, :, None], s_sel, -jnp.inf)
        ko = lax.dynamic_slice_in_dim(kp, blk * MOBA_BLOCK, MOBA_BLOCK, axis=2)
        vo = lax.dynamic_slice_in_dim(vp, blk * MOBA_BLOCK, MOBA_BLOCK, axis=2)
        s_own = jnp.einsum('bhqd,bhkd->bhqk', qc, ko).astype(jnp.float32) * scale
        opos = blk * MOBA_BLOCK + jnp.arange(MOBA_BLOCK)
        s_own = jnp.where(opos[None, :] <= cpos[:, None], s_own, -jnp.inf)
        s = jnp.concatenate([s_sel.reshape(B, N_HEADS, MOBA_Q_CHUNK, n_sel * MOBA_BLOCK), s_own], axis=-1)
        p = jax.nn.softmax(s, axis=-1).astype(v.dtype)
        p_sel = p[..., :n_sel * MOBA_BLOCK].reshape(B, N_HEADS, MOBA_Q_CHUNK, n_sel, MOBA_BLOCK)
        p_own = p[..., n_sel * MOBA_BLOCK:]
        return (jnp.einsum('bhqjk,bhqjkd->bhqd', p_sel, vg)
                + jnp.einsum('bhqk,bhkd->bhqd', p_own, vo))

    o = lax.map(chunk, jnp.arange(S // MOBA_Q_CHUNK))
    return o.transpose(1, 0, 3, 2, 4).reshape(B, S, D_MODEL)


def dsa_attention(qkv, hn, cos, sin, w_iq, w_ik, w_iw):
    B, S, _ = qkv.shape
    q, k, v = jnp.split(qkv, 3, axis=-1)
    q = apply_rope(q.reshape(B, S, N_HEADS, HEAD_DIM), cos, sin)
    k = apply_rope(k.reshape(B, S, N_HEADS, HEAD_DIM), cos, sin)
    v = v.reshape(B, S, N_HEADS, HEAD_DIM)
    qi = apply_rope((hn @ w_iq).reshape(B, S, IDX_HEADS, IDX_DIM), cos, sin)
    ki = apply_rope((hn @ w_ik).reshape(B, S, 1, IDX_DIM), cos, sin)[:, :, 0]
    wts = (hn @ w_iw).astype(jnp.float32) * (IDX_HEADS ** -0.5 * IDX_DIM ** -0.5)
    n_sel = min(DSA_TOPK_MAX, S // 4)
    kpos = jnp.arange(S)
    scale = HEAD_DIM ** -0.5
    gather_rows = jax.vmap(lambda arr, ii: arr[ii])

    def chunk(c):
        q0 = c * DSA_Q_CHUNK
        qc = lax.dynamic_slice_in_dim(q, q0, DSA_Q_CHUNK, axis=1)
        qic = lax.dynamic_slice_in_dim(qi, q0, DSA_Q_CHUNK, axis=1)
        wc = lax.dynamic_slice_in_dim(wts, q0, DSA_Q_CHUNK, axis=1)
        cpos = q0 + jnp.arange(DSA_Q_CHUNK)
        dots = jnp.einsum('bqhd,bsd->bqhs', qic, ki).astype(jnp.float32)
        idx_score = jnp.einsum('bqhs,bqh->bqs', jax.nn.relu(dots), wc)
        idx_score = jnp.where(kpos[None, None, :] <= cpos[None, :, None], idx_score, -jnp.inf)
        _, sel = lax.top_k(idx_score, n_sel)
        valid = sel <= cpos[None, :, None]
        kg = gather_rows(k, sel)
        vg = gather_rows(v, sel)
        s = jnp.einsum('bqhd,bqjhd->bhqj', qc, kg).astype(jnp.float32) * scale
        s = jnp.where(valid[:, None], s, -jnp.inf)
        p = jax.nn.softmax(s, axis=-1).astype(v.dtype)
        return jnp.einsum('bhqj,bqjhd->bqhd', p, vg)

    o = lax.map(chunk, jnp.arange(S // DSA_Q_CHUNK))
    return o.transpose(1, 0, 2, 3, 4).reshape(B, S, D_MODEL)


def setup_inputs(seed: int = 0) -> dict:
    key = jax.random.key(seed)
    ks = jax.random.split(key, 20)
    f32 = jnp.float32
    d_s = D_MODEL ** -0.5
    x = jax.random.normal(ks[0], (BATCH, SEQ, D_MODEL), f32)
    positions = jnp.tile(jnp.arange(SEQ, dtype=jnp.int32)[None, :], (BATCH, 1))
    norm_g = 1.0 + 0.02 * jax.random.normal(ks[1], (DEPTH, 3, D_MODEL), f32)
    ffn_w_gate = jax.random.normal(ks[2], (DEPTH, 2, D_MODEL, D_FF), f32) * d_s
    ffn_w_up = jax.random.normal(ks[3], (DEPTH, 2, D_MODEL, D_FF), f32) * d_s
    ffn_w_down = jax.random.normal(ks[4], (DEPTH, 2, D_FF, D_MODEL), f32) * (D_FF ** -0.5)
    attn_w_in = jax.random.normal(ks[5], (DEPTH, D_MODEL, 3 * D_MODEL), f32) * d_s
    attn_w_out = jax.random.normal(ks[6], (DEPTH, D_MODEL, D_MODEL), f32) * d_s
    diff_lambda_q1 = 0.1 * jax.random.normal(ks[7], (N_LAYERS_A, HEAD_DIM), f32)
    diff_lambda_k1 = 0.1 * jax.random.normal(ks[8], (N_LAYERS_A, HEAD_DIM), f32)
    diff_lambda_q2 = 0.1 * jax.random.normal(ks[9], (N_LAYERS_A, HEAD_DIM), f32)
    diff_lambda_k2 = 0.1 * jax.random.normal(ks[10], (N_LAYERS_A, HEAD_DIM), f32)
    diff_subln_g = 1.0 + 0.02 * jax.random.normal(ks[11], (N_LAYERS_A, 2 * HEAD_DIM), f32)
    idx_w_q = jax.random.normal(ks[12], (N_LAYERS_C, D_MODEL, IDX_HEADS * IDX_DIM), f32) * d_s
    idx_w_k = jax.random.normal(ks[13], (N_LAYERS_C, D_MODEL, IDX_DIM), f32) * d_s
    idx_w_head = jax.random.normal(ks[14], (N_LAYERS_C, D_MODEL, IDX_HEADS), f32) * d_s
    final_norm_g = 1.0 + 0.02 * jax.random.normal(ks[15], (D_MODEL,), f32)
    return {"x": x, "positions": positions, "norm_g": norm_g,
            "ffn_w_gate": ffn_w_gate, "ffn_w_up": ffn_w_up, "ffn_w_down": ffn_w_down,
            "attn_w_in": attn_w_in, "attn_w_out": attn_w_out,
            "diff_lambda_q1": diff_lambda_q1, "diff_lambda_k1": diff_lambda_k1,
            "diff_lambda_q2": diff_lambda_q2, "diff_lambda_k2": diff_lambda_k2,
            "diff_subln_g": diff_subln_g,
            "idx_w_q": idx_w_q, "idx_w_k": idx_w_k, "idx_w_head": idx_w_head,
            "final_norm_g": final_norm_g}


def reference(x, positions, norm_g, ffn_w_gate, ffn_w_up, ffn_w_down, attn_w_in, attn_w_out,
              diff_lambda_q1, diff_lambda_k1, diff_lambda_q2, diff_lambda_k2, diff_subln_g,
              idx_w_q, idx_w_k, idx_w_head, final_norm_g):
    cos, sin = rope_tables(positions)
    h = x
    for i in range(DEPTH):
        h = h + 0.5 * swiglu(rms_norm(h, norm_g[i, 0]), ffn_w_gate[i, 0], ffn_w_up[i, 0], ffn_w_down[i, 0])
        hn = rms_norm(h, norm_g[i, 1])
        qkv = hn @ attn_w_in[i]
        mixer = i % N_MIXERS
        j = i // N_MIXERS
        if mixer == 0:
            mix = diff_attention(qkv, cos, sin, diff_lambda_q1[j], diff_lambda_k1[j],
                                 diff_lambda_q2[j], diff_lambda_k2[j], diff_subln_g[j],
                                 lambda_init_for(i))
        elif mixer == 1:
            mix = moba_attention(qkv, cos, sin)
        else:
            mix = dsa_attention(qkv, hn, cos, sin, idx_w_q[j], idx_w_k[j], idx_w_head[j])
        h = h + mix @ attn_w_out[i]
        h = h + 0.5 * swiglu(rms_norm(h, norm_g[i, 2]), ffn_w_gate[i, 1], ffn_w_up[i, 1], ffn_w_down[i, 1])
    return rms_norm(h, final_norm_g)
```

```python
import functools
import math

import jax
import jax.numpy as jnp
from jax import lax
from jax.experimental import pallas as pl
from jax.experimental.pallas import tpu as pltpu

F32 = jnp.float32
BF16 = jnp.bfloat16
I32 = jnp.int32

HEAD_DIM = 128
ROT_DIM = HEAD_DIM // 4
ROT_HALF = ROT_DIM // 2
ROPE_THETA = 500000.0
RMS_EPS = 1e-6
SUBLN_EPS = 1e-5
MOBA_BLOCK = 256
MOBA_TOPK = 3
DSA_TOPK_MAX = 256
N_MIXERS = 3

LANES = 128
MIB = 1024 * 1024
VMEM_CAP_BYTES = 60 * MIB
NEG_INF = float("-inf")
INT_MIN = -(2 ** 31)


def _cparams(n_axes, vmem_bytes):
    return pltpu.CompilerParams(
        dimension_semantics=("arbitrary",) * n_axes,
        vmem_limit_bytes=int(min(VMEM_CAP_BYTES, vmem_bytes)))


def _pick(n, candidates):
    for c in candidates:
        if n % c == 0:
            return c
    raise ValueError(f"no tile in {candidates} divides {n}")


def _rmsnorm_kernel(x_ref, g_ref, o_ref, *, eps):
    x = x_ref[...]
    y = x * lax.rsqrt(jnp.mean(x * x, axis=-1, keepdims=True) + eps) * g_ref[...]
    o_ref[...] = y.astype(o_ref.dtype)


def _rmsnorm(x, g, out_dtype):
    m, d = x.shape
    tm = _pick(m, (256, 128, 8))
    vmem = 2 * tm * d * (4 + jnp.dtype(out_dtype).itemsize) + 4 * tm * d * 4 + 4 * MIB
    return pl.pallas_call(
        functools.partial(_rmsnorm_kernel, eps=RMS_EPS),
        out_shape=jax.ShapeDtypeStruct((m, d), out_dtype),
        grid=(m // tm,),
        in_specs=[pl.BlockSpec((tm, d), lambda i: (i, 0)),
                  pl.BlockSpec((1, d), lambda i: (0, 0))],
        out_specs=pl.BlockSpec((tm, d), lambda i: (i, 0)),
        compiler_params=_cparams(1, vmem),
        name="rmsnorm",
    )(x, g.reshape(1, d))


def _rope_table_kernel(pos_ref, inv_ref, c_ref, s1_ref, s2_ref):
    ang = pos_ref[...].astype(F32) * inv_ref[...]
    lane = lax.broadcasted_iota(I32, ang.shape, 1)
    cos = jnp.cos(ang)
    sin = jnp.sin(ang)
    c_ref[...] = jnp.where(lane < ROT_DIM, cos, 1.0)
    s1_ref[...] = jnp.where(lane < ROT_HALF, 0.0, jnp.where(lane < ROT_DIM, sin, 0.0))
    s2_ref[...] = jnp.where(lane < ROT_HALF, -sin, 0.0)


def _rope_tables(positions):
    m = positions.size
    inv = ROPE_THETA ** (-jnp.arange(0, ROT_DIM, 2, dtype=F32) / ROT_DIM)
    inv = jnp.concatenate([inv, inv, jnp.zeros((LANES - ROT_DIM,), F32)]).reshape(1, LANES)
    tm = _pick(m, (512, 256, 128, 8))
    spec = pl.BlockSpec((tm, LANES), lambda i: (i, 0))
    return pl.pallas_call(
        _rope_table_kernel,
        out_shape=[jax.ShapeDtypeStruct((m, LANES), F32)] * 3,
        grid=(m // tm,),
        in_specs=[pl.BlockSpec((tm, 1), lambda i: (i, 0)),
                  pl.BlockSpec((1, LANES), lambda i: (0, 0))],
        out_specs=[spec, spec, spec],
        compiler_params=_cparams(1, 16 * MIB),
        name="rope_tables",
    )(positions.reshape(m, 1).astype(I32), inv)


def _rope(a, c, s1, s2):
    return (a * c + pltpu.roll(a, ROT_HALF, 1) * s1
            + pltpu.roll(a, LANES - ROT_HALF, 1) * s2)


def _mm_vmem(tm, tn, k, out_bytes, n_w=1, extra=0):
    return (2 * tm * k * 2 + 2 * n_w * k * tn * 2 + 2 * tm * tn * out_bytes
            + (n_w + 1) * tm * tn * 4 + extra + 4 * MIB)


def _mm_rope_kernel(x_ref, w_ref, c_ref, s1_ref, s2_ref, *o_refs, tn, n_rope_tiles, splits):
    j = pl.program_id(1)
    acc = jnp.dot(x_ref[...], w_ref[...], preferred_element_type=F32)

    def emit(val):
        for oi, lo, hi, scale in splits:
            piece = val[:, lo:hi]
            if scale != 1.0:
                piece = piece * scale
            o_refs[oi][...] = piece.astype(o_refs[oi].dtype)

    if n_rope_tiles == 0:
        emit(acc)
        return

    def roped():
        c, s1, s2 = c_ref[...], s1_ref[...], s2_ref[...]
        heads = [_rope(acc[:, g * LANES:(g + 1) * LANES], c, s1, s2) for g in range(tn // LANES)]
        return heads[0] if len(heads) == 1 else jnp.concatenate(heads, axis=1)

    @pl.when(j < n_rope_tiles)
    def _():
        emit(roped())

    @pl.when(j >= n_rope_tiles)
    def _():
        emit(acc)


def _matmul_rope(x, w, tables, n_rope_cols, out_dtype=BF16):
    m, k = x.shape
    n = w.shape[1]
    tm = _pick(m, (1024, 512, 256, 128))
    tn = _pick(n, (512, 256, 128))
    assert n_rope_cols % tn == 0
    tab_spec = pl.BlockSpec((tm, LANES), lambda i, j: (i, 0))
    kern = functools.partial(_mm_rope_kernel, tn=tn, n_rope_tiles=n_rope_cols // tn,
                             splits=((0, 0, tn, 1.0),))
    return pl.pallas_call(
        kern,
        out_shape=jax.ShapeDtypeStruct((m, n), out_dtype),
        grid=(m // tm, n // tn),
        in_specs=[pl.BlockSpec((tm, k), lambda i, j: (i, 0)),
                  pl.BlockSpec((k, tn), lambda i, j: (0, j)),
                  tab_spec, tab_spec, tab_spec],
        out_specs=pl.BlockSpec((tm, tn), lambda i, j: (i, j)),
        compiler_params=_cparams(2, _mm_vmem(tm, tn, k, 2, extra=6 * tm * LANES * 4)),
        name="matmul_rope",
    )(x, w, *tables)


def _index_kw_proj(x, w_k, w_head, tables, head_scale):
    m, k = x.shape
    nh = w_head.shape[1]
    assert w_k.shape[1] == LANES and nh <= LANES
    w = jnp.concatenate([w_k, w_head, jnp.zeros((k, LANES - nh), w_k.dtype)], axis=1)
    tm = _pick(m, (1024, 512, 256, 128))
    tab_spec = pl.BlockSpec((tm, LANES), lambda i, j: (i, 0))
    out_spec = pl.BlockSpec((tm, LANES), lambda i, j: (i, 0))

    def kern(x_ref, w_ref, c_ref, s1_ref, s2_ref, ki_ref, wt_ref):
        acc = jnp.dot(x_ref[...], w_ref[...], preferred_element_type=F32)
        ki_ref[...] = _rope(acc[:, :LANES], c_ref[...], s1_ref[...], s2_ref[...]).astype(BF16)
        wt_ref[...] = acc[:, LANES:] * head_scale

    return pl.pallas_call(
        kern,
        out_shape=[jax.ShapeDtypeStruct((m, LANES), BF16), jax.ShapeDtypeStruct((m, LANES), F32)],
        grid=(m // tm, 1),
        in_specs=[pl.BlockSpec((tm, k), lambda i, j: (i, 0)),
                  pl.BlockSpec((k, 2 * LANES), lambda i, j: (0, 0)),
                  tab_spec, tab_spec, tab_spec],
        out_specs=[out_spec, out_spec],
        compiler_params=_cparams(2, _mm_vmem(tm, 2 * LANES, k, 4, extra=6 * tm * LANES * 4)),
        name="index_kw_proj",
    )(x, w, *tables)


def _index_q_proj(x, w, tables):
    m, k = x.shape
    n = w.shape[1]
    nh = n // LANES
    tm = _pick(m, (1024, 512, 256, 128))
    tn = _pick(n, (512, 256, 128))
    hpt = tn // LANES
    tab_spec = pl.BlockSpec((tm, LANES), lambda i, j: (i, 0))

    def kern(x_ref, w_ref, c_ref, s1_ref, s2_ref, o_ref):
        acc = jnp.dot(x_ref[...], w_ref[...], preferred_element_type=F32)
        c, s1, s2 = c_ref[...], s1_ref[...], s2_ref[...]
        for g in range(hpt):
            o_ref[g] = _rope(acc[:, g * LANES:(g + 1) * LANES], c, s1, s2).astype(BF16)

    return pl.pallas_call(
        kern,
        out_shape=jax.ShapeDtypeStruct((nh, m, LANES), BF16),
        grid=(m // tm, n // tn),
        in_specs=[pl.BlockSpec((tm, k), lambda i, j: (i, 0)),
                  pl.BlockSpec((k, tn), lambda i, j: (0, j)),
                  tab_spec, tab_spec, tab_spec],
        out_specs=pl.BlockSpec((hpt, tm, LANES), lambda i, j: (j, i, 0)),
        compiler_params=_cparams(2, _mm_vmem(tm, tn, k, 2, extra=6 * tm * LANES * 4)),
        name="index_q_proj",
    )(x, w, *tables)


def _swiglu_kernel(x_ref, wg_ref, wu_ref, o_ref):
    x = x_ref[...]
    g = jnp.dot(x, wg_ref[...], preferred_element_type=F32)
    u = jnp.dot(x, wu_ref[...], preferred_element_type=F32)
    o_ref[...] = (g * jax.nn.sigmoid(g) * u).astype(o_ref.dtype)


def _swiglu_up(x, wg, wu):
    m, k = x.shape
    n = wg.shape[1]
    tm = _pick(m, (1024, 512, 256, 128))
    tn = _pick(n, (512, 256, 128))
    return pl.pallas_call(
        _swiglu_kernel,
        out_shape=jax.ShapeDtypeStruct((m, n), BF16),
        grid=(m // tm, n // tn),
        in_specs=[pl.BlockSpec((tm, k), lambda i, j: (i, 0)),
                  pl.BlockSpec((k, tn), lambda i, j: (0, j)),
                  pl.BlockSpec((k, tn), lambda i, j: (0, j))],
        out_specs=pl.BlockSpec((tm, tn), lambda i, j: (i, j)),
        compiler_params=_cparams(2, _mm_vmem(tm, tn, k, 2, n_w=2)),
        name="swiglu_up",
    )(x, wg, wu)


def _mm_residual_kernel(x_ref, w_ref, r_ref, o_ref, *, scale):
    acc = jnp.dot(x_ref[...], w_ref[...], preferred_element_type=F32)
    if scale != 1.0:
        acc = acc * scale
    o_ref[...] = r_ref[...] + acc


def _matmul_residual(x, w, res, scale):
    m, k = x.shape
    n = w.shape[1]
    tm = _pick(m, (1024, 512, 256, 128))
    tn = _pick(n, (512, 256, 128))
    return pl.pallas_call(
        functools.partial(_mm_residual_kernel, scale=scale),
        out_shape=jax.ShapeDtypeStruct((m, n), F32),
        grid=(m // tm, n // tn),
        in_specs=[pl.BlockSpec((tm, k), lambda i, j: (i, 0)),
                  pl.BlockSpec((k, tn), lambda i, j: (0, j)),
                  pl.BlockSpec((tm, tn), lambda i, j: (i, j))],
        out_specs=pl.BlockSpec((tm, tn), lambda i, j: (i, j)),
        compiler_params=_cparams(2, _mm_vmem(tm, tn, k, 4, extra=2 * tm * tn * 4)),
        name="matmul_residual",
    )(x, w, res)


def _flash_init(m_ref, l_ref, acc_ref):
    m_ref[...] = jnp.full(m_ref.shape, NEG_INF, F32)
    l_ref[...] = jnp.zeros(l_ref.shape, F32)
    acc_ref[...] = jnp.zeros(acc_ref.shape, F32)


def _flash_step(q, k, v, mask, m_ref, l_ref, acc_ref, scale):
    s = lax.dot_general(q, k, (((1,), (1,)), ((), ())), preferred_element_type=F32) * scale
    if mask is not None:
        s = jnp.where(mask, s, NEG_INF)
    m_prev = m_ref[...]
    m_new = jnp.maximum(m_prev, jnp.max(s, axis=-1, keepdims=True))
    m_safe = jnp.where(m_new == NEG_INF, 0.0, m_new)
    alpha = jnp.exp(m_prev - m_safe)
    p = jnp.exp(s - m_safe)
    l_ref[...] = alpha * l_ref[...] + jnp.sum(p, axis=-1, keepdims=True)
    acc_ref[...] = alpha * acc_ref[...] + jnp.dot(p.astype(BF16), v, preferred_element_type=F32)
    m_ref[...] = m_new


def _causal_mask(t):
    row = lax.broadcasted_iota(I32, (t, t), 0)
    col = lax.broadcasted_iota(I32, (t, t), 1)
    return col <= row


def _diff_attn_kernel(lam_ref, g_ref, q_ref, k_ref, v_ref, o_ref, m_ref, l_ref, acc_ref,
                      *, t, scale, lam_init):
    i = pl.program_id(2)
    _flash_init(m_ref, l_ref, acc_ref)
    q = q_ref[...]

    def chunk(j, mask):
        off = pl.multiple_of(j * t, t)
        v = v_ref[pl.ds(off, t), :]
        for c in range(2):
            k = k_ref[pl.ds(off, t), c * HEAD_DIM:(c + 1) * HEAD_DIM]
            _flash_step(q[:, c * HEAD_DIM:(c + 1) * HEAD_DIM], k, v, mask,
                        m_ref.at[c], l_ref.at[c], acc_ref.at[c], scale)

    chunk(i, _causal_mask(t))

    def body(j, carry):
        chunk(j, None)
        return carry

    lax.fori_loop(0, i, body, 0)

    p = lam_ref[...]
    lam = (jnp.exp(jnp.sum(p[0:1] * p[1:2], axis=-1, keepdims=True))
           - jnp.exp(jnp.sum(p[2:3] * p[3:4], axis=-1, keepdims=True)) + lam_init)
    o = acc_ref[0] * (1.0 / l_ref[0]) - acc_ref[1] * (lam / l_ref[1])
    y = o * lax.rsqrt(jnp.mean(o * o, axis=-1, keepdims=True) + SUBLN_EPS) * g_ref[...]
    o_ref[...] = (y * (1.0 - lam_init)).astype(o_ref.dtype)


def _diff_attention(qkv, lam_params, subln_g, lam_init, batch, seq):
    m, three_d = qkv.shape
    d = three_d // 3
    dh = 2 * HEAD_DIM
    nh = d // dh
    t = _pick(seq, (512, 256, 128))
    nq = seq // t
    vmem = 2 * (2 * seq * dh * 2) + 4 * t * dh * 2 + 3 * t * dh * 4 + 8 * t * t * 4 + 6 * MIB
    kern = functools.partial(_diff_attn_kernel, t=t, scale=HEAD_DIM ** -0.5, lam_init=lam_init)
    return pl.pallas_call(
        kern,
        out_shape=jax.ShapeDtypeStruct((m, d), BF16),
        grid=(batch, nh, nq),
        in_specs=[pl.BlockSpec((4, HEAD_DIM), lambda b, h, i: (0, 0)),
                  pl.BlockSpec((1, dh), lambda b, h, i: (0, 0)),
                  pl.BlockSpec((t, dh), lambda b, h, i: (b * nq + i, h)),
                  pl.BlockSpec((seq, dh), lambda b, h, i: (b, nh + h)),
                  pl.BlockSpec((seq, dh), lambda b, h, i: (b, 2 * nh + h))],
        out_specs=pl.BlockSpec((t, dh), lambda b, h, i: (b * nq + i, h)),
        scratch_shapes=[pltpu.VMEM((2, t, 1), F32), pltpu.VMEM((2, t, 1), F32),
                        pltpu.VMEM((2, t, dh), F32)],
        compiler_params=_cparams(3, vmem),
        name="diff_attention",
    )(lam_params, subln_g.reshape(1, dh), qkv, qkv, qkv)


def _moba_attn_kernel(q_ref, k_ref, v_ref, o_ref, kmean_ref, sel_ref, m_ref, l_ref, acc_ref,
                      *, t, blk, nbp, topk, scale):
    i = pl.program_id(2)
    seq = k_ref.shape[0]
    bpc = t // blk
    blk_shift = blk.bit_length() - 1

    @pl.when(i == 0)
    def _():
        r = lax.broadcasted_iota(I32, (nbp, seq), 0)
        c = lax.broadcasted_iota(I32, (nbp, seq), 1)
        avg = jnp.where(c >= r * blk, jnp.where(c < (r + 1) * blk, 1.0 / blk, 0.0), 0.0).astype(BF16)
        kmean_ref[...] = jnp.dot(avg, k_ref[...], preferred_element_type=F32).astype(BF16)

    q = q_ref[...]
    gate = lax.dot_general(q, kmean_ref[...], (((1,), (1,)), ((), ())), preferred_element_type=F32)
    n_idx = lax.broadcasted_iota(I32, (t, nbp), 1)
    qblk = lax.shift_right_logical(i * t + lax.broadcasted_iota(I32, (t, nbp), 0), blk_shift)
    valid = n_idx < qblk
    g = jnp.where(valid, gate, NEG_INF)
    picked = jnp.zeros((t, nbp), F32)
    for _ in range(topk):
        mx = jnp.max(g, axis=-1, keepdims=True)
        first = jnp.min(jnp.where(g == mx, n_idx, nbp), axis=-1, keepdims=True)
        hit = n_idx == first
        picked = jnp.where(hit, 1.0, picked)
        g = jnp.where(hit, NEG_INF, g)
    sel_ref[...] = jnp.where(valid, picked, 0.0)

    _flash_init(m_ref, l_ref, acc_ref)
    col_blk = lax.shift_right_logical(lax.broadcasted_iota(I32, (t, t), 1), blk_shift)

    def selected(j):
        sel = sel_ref[...]
        out = None
        for b in range(bpc):
            col = jnp.sum(jnp.where(n_idx == j * bpc + b, sel, 0.0), axis=-1, keepdims=True)
            out = col if out is None else jnp.where(col_blk >= b, col, out)
        return jnp.broadcast_to(out, (t, t))

    def chunk(j, mask):
        off = pl.multiple_of(j * t, t)
        _flash_step(q, k_ref[pl.ds(off, t), :], v_ref[pl.ds(off, t), :], mask,
                    m_ref, l_ref, acc_ref, scale)

    row_blk = lax.shift_right_logical(lax.broadcasted_iota(I32, (t, t), 0), blk_shift)
    own_or_picked = jnp.where(row_blk == col_blk, 1.0, selected(i))
    chunk(i, jnp.where(_causal_mask(t), own_or_picked, 0.0) > 0.5)

    def body(j, carry):
        chunk(j, selected(j) > 0.5)
        return carry

    lax.fori_loop(0, i, body, 0)
    o_ref[...] = (acc_ref[...] * (1.0 / l_ref[...])).astype(o_ref.dtype)


def _moba_attention(qkv, batch, seq):
    m, three_d = qkv.shape
    d = three_d // 3
    nh = d // HEAD_DIM
    assert seq % MOBA_BLOCK == 0 and MOBA_BLOCK & (MOBA_BLOCK - 1) == 0
    t = _pick(seq, (512, 256))
    assert t % MOBA_BLOCK == 0
    nq = seq // t
    nb = seq // MOBA_BLOCK
    nbp = -(-nb // 16) * 16
    vmem = 2 * (2 * seq * HEAD_DIM * 2) + 4 * t * HEAD_DIM * 2 + 3 * t * LANES * 4 + 10 * t * t * 4 + 6 * MIB
    kern = functools.partial(_moba_attn_kernel, t=t, blk=MOBA_BLOCK, nbp=nbp,
                             topk=min(MOBA_TOPK, nb), scale=HEAD_DIM ** -0.5)
    return pl.pallas_call(
        kern,
        out_shape=jax.ShapeDtypeStruct((m, d), BF16),
        grid=(batch, nh, nq),
        in_specs=[pl.BlockSpec((t, HEAD_DIM), lambda b, h, i: (b * nq + i, h)),
                  pl.BlockSpec((seq, HEAD_DIM), lambda b, h, i: (b, nh + h)),
                  pl.BlockSpec((seq, HEAD_DIM), lambda b, h, i: (b, 2 * nh + h))],
        out_specs=pl.BlockSpec((t, HEAD_DIM), lambda b, h, i: (b * nq + i, h)),
        scratch_shapes=[pltpu.VMEM((nbp, HEAD_DIM), BF16), pltpu.VMEM((t, nbp), F32),
                        pltpu.VMEM((t, 1), F32), pltpu.VMEM((t, 1), F32),
                        pltpu.VMEM((t, HEAD_DIM), F32)],
        compiler_params=_cparams(3, vmem),
        name="moba_attention",
    )(qkv, qkv, qkv)


def _dsa_index_kernel(qi_ref, ki_ref, w_ref, mask_ref, key_ref, wb_ref, *, tq, tk, nh, topk):
    i = pl.program_id(1)
    nkc = key_ref.shape[0]
    w = w_ref[...]
    for h in range(nh):
        wb_ref[h] = jnp.broadcast_to(w[:, h:h + 1], (tq, LANES))
    key_ref[...] = jnp.full(key_ref.shape, INT_MIN, I32)

    q_all = qi_ref[...].reshape(nh * tq, HEAD_DIM)
    qpos = i * tq + lax.broadcasted_iota(I32, (tq, tk), 0)
    col = lax.broadcasted_iota(I32, (tq, tk), 1)
    n_chunks = (i * tq + tq + tk - 1) // tk

    def score_chunk(c, carry):
        off = pl.multiple_of(c * tk, tk)
        kc = ki_ref[pl.ds(off, tk), :]
        r = lax.dot_general(q_all, kc, (((1,), (1,)), ((), ())), preferred_element_type=F32)
        acc = jnp.zeros((tq, tk), F32)
        for h in range(nh):
            wbh = wb_ref[h]
            wfull = jnp.concatenate([wbh] * (tk // LANES), axis=1)
            acc = acc + jnp.maximum(r[h * tq:(h + 1) * tq], 0.0) * wfull
        bits = lax.bitcast_convert_type(acc, I32)
        key = bits ^ (lax.shift_right_arithmetic(bits, 31) & 0x7FFFFFFF)
        key_ref[c] = jnp.where(off + col <= qpos, key, INT_MIN)
        return carry

    lax.fori_loop(0, n_chunks, score_chunk, 0)

    def count_ge(cand):
        ge = jnp.where(key_ref[...] >= cand, 1.0, 0.0)
        return jnp.sum(jnp.sum(ge, axis=0), axis=-1, keepdims=True)

    zero = jnp.zeros((tq, 1), I32)
    thr = jnp.where(count_ge(zero) >= topk, zero, INT_MIN)

    def bisect(b, thr):
        cand = thr + lax.shift_left(jnp.int32(1), 30 - b)
        return jnp.where(count_ge(cand) >= topk, cand, thr)

    thr = lax.fori_loop(0, 31, bisect, thr)
    thr = jnp.maximum(thr, INT_MIN + 1)
    mask_ref[...] = jnp.where(key_ref[...] >= thr, 1.0, 0.0).astype(mask_ref.dtype)


def _dsa_index_mask(qi, ki, wts, batch, seq, tk):
    nh, m, _ = qi.shape
    tq = _pick(seq, (128,))
    nq = seq // tq
    nkc = seq // tk
    topk = min(DSA_TOPK_MAX, seq // 4)
    vmem = (2 * nh * tq * HEAD_DIM * 2 + 2 * seq * HEAD_DIM * 2 + 2 * tq * LANES * 4
            + 2 * tq * seq * 2 + tq * seq * 4 + nh * tq * LANES * 4
            + 2 * nh * tq * tk * 4 + 4 * tq * seq * 4 + 6 * MIB)
    kern = functools.partial(_dsa_index_kernel, tq=tq, tk=tk, nh=nh, topk=topk)
    return pl.pallas_call(
        kern,
        out_shape=jax.ShapeDtypeStruct((batch, nkc, seq, tk), BF16),
        grid=(batch, nq),
        in_specs=[pl.BlockSpec((nh, tq, HEAD_DIM), lambda b, i: (0, b * nq + i, 0)),
                  pl.BlockSpec((seq, HEAD_DIM), lambda b, i: (b, 0)),
                  pl.BlockSpec((tq, LANES), lambda b, i: (b * nq + i, 0))],
        out_specs=pl.BlockSpec((None, nkc, tq, tk), lambda b, i: (b, 0, i, 0)),
        scratch_shapes=[pltpu.VMEM((nkc, tq, tk), I32), pltpu.VMEM((nh, tq, LANES), F32)],
        compiler_params=_cparams(2, vmem),
        name="dsa_index_mask",
    )(qi, ki, wts)


def _dsa_attn_kernel(q_ref, k_ref, v_ref, mask_ref, o_ref, m_ref, l_ref, acc_ref, *, t, scale):
    i = pl.program_id(1)
    _flash_init(m_ref, l_ref, acc_ref)
    q = q_ref[...]

    def body(j, carry):
        off = pl.multiple_of(j * t, t)
        mask = mask_ref[j].astype(F32) > 0.5
        _flash_step(q, k_ref[pl.ds(off, t), :], v_ref[pl.ds(off, t), :], mask,
                    m_ref, l_ref, acc_ref, scale)
        return carry

    lax.fori_loop(0, i + 1, body, 0)
    o_ref[...] = (acc_ref[...] * (1.0 / l_ref[...])).astype(o_ref.dtype)


def _dsa_attention(qkv, mask, batch, seq, t):
    m, three_d = qkv.shape
    d = three_d // 3
    nh = d // HEAD_DIM
    nq = seq // t
    nkc = seq // t
    vmem = (2 * (2 * seq * HEAD_DIM * 2) + 4 * t * HEAD_DIM * 2 + 2 * nkc * t * t * 2
            + 3 * t * LANES * 4 + 10 * t * t * 4 + 6 * MIB)
    kern = functools.partial(_dsa_attn_kernel, t=t, scale=HEAD_DIM ** -0.5)
    return pl.pallas_call(
        kern,
        out_shape=jax.ShapeDtypeStruct((m, d), BF16),
        grid=(batch, nq, nh),
        in_specs=[pl.BlockSpec((t, HEAD_DIM), lambda b, i, h: (b * nq + i, h)),
                  pl.BlockSpec((seq, HEAD_DIM), lambda b, i, h: (b, nh + h)),
                  pl.BlockSpec((seq, HEAD_DIM), lambda b, i, h: (b, 2 * nh + h)),
                  pl.BlockSpec((None, nkc, t, t), lambda b, i, h: (b, 0, i, 0))],
        out_specs=pl.BlockSpec((t, HEAD_DIM), lambda b, i, h: (b * nq + i, h)),
        scratch_shapes=[pltpu.VMEM((t, 1), F32), pltpu.VMEM((t, 1), F32),
                        pltpu.VMEM((t, HEAD_DIM), F32)],
        compiler_params=_cparams(3, vmem),
        name="dsa_attention",
    )(qkv, qkv, qkv, mask)


def _lambda_init_for(layer):
    return 0.8 - 0.6 * math.exp(-0.3 * layer)


def _ffn(h, g, w_gate, w_up, w_down):
    act = _swiglu_up(_rmsnorm(h, g, BF16), w_gate.astype(BF16), w_up.astype(BF16))
    return _matmul_residual(act, w_down.astype(BF16), h, 0.5)


def kernel(x, positions, norm_g, ffn_w_gate, ffn_w_up, ffn_w_down, attn_w_in, attn_w_out,
           diff_lambda_q1, diff_lambda_k1, diff_lambda_q2, diff_lambda_k2, diff_subln_g,
           idx_w_q, idx_w_k, idx_w_head, final_norm_g):
    batch, seq, d = x.shape
    depth = norm_g.shape[0]
    m = batch * seq
    tables = _rope_tables(positions)
    h = x.reshape(m, d)
    for i in range(depth):
        h = _ffn(h, norm_g[i, 0], ffn_w_gate[i, 0], ffn_w_up[i, 0], ffn_w_down[i, 0])
        hn = _rmsnorm(h, norm_g[i, 1], BF16)
        qkv = _matmul_rope(hn, attn_w_in[i].astype(BF16), tables, 2 * d)
        mixer = i % N_MIXERS
        j = i // N_MIXERS
        if mixer == 0:
            lam_params = jnp.stack([diff_lambda_q1[j], diff_lambda_k1[j],
                                    diff_lambda_q2[j], diff_lambda_k2[j]])
            mix = _diff_attention(qkv, lam_params, diff_subln_g[j], _lambda_init_for(i), batch, seq)
        elif mixer == 1:
            mix = _moba_attention(qkv, batch, seq)
        else:
            n_idx_heads = idx_w_head.shape[-1]
            idx_dim = idx_w_k.shape[-1]
            qi = _index_q_proj(hn, idx_w_q[j].astype(BF16), tables)
            ki, wts = _index_kw_proj(hn, idx_w_k[j].astype(BF16), idx_w_head[j].astype(BF16),
                                     tables, n_idx_heads ** -0.5 * idx_dim ** -0.5)
            t = _pick(seq, (512, 256))
            mask = _dsa_index_mask(qi, ki, wts, batch, seq, t)
            mix = _dsa_attention(qkv, mask, batch, seq, t)
        h = _matmul_residual(mix, attn_w_out[i].astype(BF16), h, 1.0)
        h = _ffn(h, norm_g[i, 2], ffn_w_gate[i, 1], ffn_w_up[i, 1], ffn_w_down[i, 1])
    return _rmsnorm(h, final_norm_g, x.dtype).reshape(batch, seq, d)
```

```python
import functools
import math

import jax
import jax.numpy as jnp
from jax import lax
from jax.experimental import pallas as pl
from jax.experimental.pallas import tpu as pltpu

F32 = jnp.float32
BF16 = jnp.bfloat16
I32 = jnp.int32

HEAD_DIM = 128
ROT_DIM = HEAD_DIM // 4
ROT_HALF = ROT_DIM // 2
ROPE_THETA = 500000.0
RMS_EPS = 1e-6
SUBLN_EPS = 1e-5
MOBA_BLOCK = 256
MOBA_TOPK = 3
DSA_TOPK_MAX = 256
N_MIXERS = 3

LANES = 128
MIB = 1024 * 1024
VMEM_CAP_BYTES = 60 * MIB
NEG_INF = float("-inf")
INT_MIN = -(2 ** 31)


def _cparams(n_axes, vmem_bytes):
    return pltpu.CompilerParams(
        dimension_semantics=("arbitrary",) * n_axes,
        vmem_limit_bytes=int(min(VMEM_CAP_BYTES, vmem_bytes)))


def _pick(n, candidates):
    for c in candidates:
        if n % c == 0:
            return c
    raise ValueError(f"no tile in {candidates} divides {n}")


def _rmsnorm_kernel(x_ref, g_ref, o_ref, *, eps):
    x = x_ref[...]
    y = x * lax.rsqrt(jnp.mean(x * x, axis=-1, keepdims=True) + eps) * g_ref[...]
    o_ref[...] = y.astype(o_ref.dtype)


def _rmsnorm(x, g, out_dtype):
    m, d = x.shape
    tm = _pick(m, (256, 128, 8))
    vmem = 2 * tm * d * (4 + jnp.dtype(out_dtype).itemsize) + 4 * tm * d * 4 + 4 * MIB
    return pl.pallas_call(
        functools.partial(_rmsnorm_kernel, eps=RMS_EPS),
        out_shape=jax.ShapeDtypeStruct((m, d), out_dtype),
        grid=(m // tm,),
        in_specs=[pl.BlockSpec((tm, d), lambda i: (i, 0)),
                  pl.BlockSpec((1, d), lambda i: (0, 0))],
        out_specs=pl.BlockSpec((tm, d), lambda i: (i, 0)),
        compiler_params=_cparams(1, vmem),
        name="rmsnorm",
    )(x, g.reshape(1, d))


def _rope_table_kernel(pos_ref, inv_ref, c_ref, s1_ref, s2_ref):
    ang = pos_ref[...].astype(F32) * inv_ref[...]
    lane = lax.broadcasted_iota(I32, ang.shape, 1)
    cos = jnp.cos(ang)
    sin = jnp.sin(ang)
    c_ref[...] = jnp.where(lane < ROT_DIM, cos, 1.0)
    s1_ref[...] = jnp.where(lane < ROT_HALF, 0.0, jnp.where(lane < ROT_DIM, sin, 0.0))
    s2_ref[...] = jnp.where(lane < ROT_HALF, -sin, 0.0)


def _rope_tables(positions):
    m = positions.size
    inv = ROPE_THETA ** (-jnp.arange(0, ROT_DIM, 2, dtype=F32) / ROT_DIM)
    inv = jnp.concatenate([inv, inv, jnp.zeros((LANES - ROT_DIM,), F32)]).reshape(1, LANES)
    tm = _pick(m, (512, 256, 128, 8))
    spec = pl.BlockSpec((tm, LANES), lambda i: (i, 0))
    return pl.pallas_call(
        _rope_table_kernel,
        out_shape=[jax.ShapeDtypeStruct((m, LANES), F32)] * 3,
        grid=(m // tm,),
        in_specs=[pl.BlockSpec((tm, 1), lambda i: (i, 0)),
                  pl.BlockSpec((1, LANES), lambda i: (0, 0))],
        out_specs=[spec, spec, spec],
        compiler_params=_cparams(1, 16 * MIB),
        name="rope_tables",
    )(positions.reshape(m, 1).astype(I32), inv)


def _rope(a, c, s1, s2):
    return (a * c + pltpu.roll(a, ROT_HALF, 1) * s1
            + pltpu.roll(a, LANES - ROT_HALF, 1) * s2)


def _mm_vmem(tm, tn, k, out_bytes, n_w=1, extra=0):
    return (2 * tm * k * 2 + n_w * k * tn * (2 * 4 + 2) + 2 * tm * tn * out_bytes
            + (n_w + 1) * tm * tn * 4 + extra + 4 * MIB)


def _cast_weight_once(w_ref, wb_ref):
    @pl.when(pl.program_id(1) == 0)
    def _():
        wb_ref[...] = w_ref[...].astype(BF16)


def _mm_rope_kernel(x_ref, w_ref, c_ref, s1_ref, s2_ref, o_ref, wb_ref,
                    *, tn, n_q_tiles, n_rope_tiles, q_scale):
    j = pl.program_id(0)
    _cast_weight_once(w_ref, wb_ref)
    acc = jnp.dot(x_ref[...], wb_ref[...], preferred_element_type=F32)

    def roped(scale):
        c, s1, s2 = c_ref[...], s1_ref[...], s2_ref[...]
        for g in range(tn // LANES):
            sl = slice(g * LANES, (g + 1) * LANES)
            y = _rope(acc[:, sl], c, s1, s2)
            o_ref[:, sl] = (y if scale is None else y * scale).astype(o_ref.dtype)

    @pl.when(j < n_q_tiles)
    def _():
        roped(q_scale)

    @pl.when((j >= n_q_tiles) & (j < n_rope_tiles))
    def _():
        roped(None)

    @pl.when(j >= n_rope_tiles)
    def _():
        o_ref[...] = acc.astype(o_ref.dtype)


def _matmul_rope(x, w, tables, n_q_cols, n_rope_cols, q_scale, out_dtype=BF16):
    m, k = x.shape
    n = w.shape[1]
    tm = _pick(m, (1024, 512, 256, 128))
    tn = _pick(n, (512, 256, 128))
    assert n_rope_cols % tn == 0 and n_q_cols % tn == 0 and n_q_cols <= n_rope_cols
    tab_spec = pl.BlockSpec((tm, LANES), lambda j, i: (i, 0))
    kern = functools.partial(_mm_rope_kernel, tn=tn, n_q_tiles=n_q_cols // tn,
                             n_rope_tiles=n_rope_cols // tn, q_scale=q_scale)
    return pl.pallas_call(
        kern,
        out_shape=jax.ShapeDtypeStruct((m, n), out_dtype),
        grid=(n // tn, m // tm),
        in_specs=[pl.BlockSpec((tm, k), lambda j, i: (i, 0)),
                  pl.BlockSpec((k, tn), lambda j, i: (0, j)),
                  tab_spec, tab_spec, tab_spec],
        out_specs=pl.BlockSpec((tm, tn), lambda j, i: (i, j)),
        scratch_shapes=[pltpu.VMEM((k, tn), BF16)],
        compiler_params=_cparams(2, _mm_vmem(tm, tn, k, 2, extra=6 * tm * LANES * 4)),
        name="matmul_rope",
    )(x, w, *tables)


def _index_kw_proj(x, w_k, w_head, tables, head_scale):
    m, k = x.shape
    nh = w_head.shape[1]
    assert w_k.shape[1] == LANES and nh <= LANES
    w = jnp.concatenate([w_k, w_head, jnp.zeros((k, LANES - nh), w_k.dtype)], axis=1)
    tm = _pick(m, (1024, 512, 256, 128))
    tab_spec = pl.BlockSpec((tm, LANES), lambda j, i: (i, 0))
    out_spec = pl.BlockSpec((tm, LANES), lambda j, i: (i, 0))

    def kern(x_ref, w_ref, c_ref, s1_ref, s2_ref, ki_ref, wt_ref, wb_ref):
        _cast_weight_once(w_ref, wb_ref)
        acc = jnp.dot(x_ref[...], wb_ref[...], preferred_element_type=F32)
        ki_ref[...] = _rope(acc[:, :LANES], c_ref[...], s1_ref[...], s2_ref[...]).astype(BF16)
        wt_ref[...] = acc[:, LANES:] * head_scale

    return pl.pallas_call(
        kern,
        out_shape=[jax.ShapeDtypeStruct((m, LANES), BF16), jax.ShapeDtypeStruct((m, LANES), F32)],
        grid=(1, m // tm),
        in_specs=[pl.BlockSpec((tm, k), lambda j, i: (i, 0)),
                  pl.BlockSpec((k, 2 * LANES), lambda j, i: (0, 0)),
                  tab_spec, tab_spec, tab_spec],
        out_specs=[out_spec, out_spec],
        scratch_shapes=[pltpu.VMEM((k, 2 * LANES), BF16)],
        compiler_params=_cparams(2, _mm_vmem(tm, 2 * LANES, k, 4, extra=6 * tm * LANES * 4)),
        name="index_kw_proj",
    )(x, w, *tables)


def _index_q_proj(x, w, tables):
    m, k = x.shape
    n = w.shape[1]
    nh = n // LANES
    tm = _pick(m, (1024, 512, 256, 128))
    tn = _pick(n, (512, 256, 128))
    hpt = tn // LANES
    tab_spec = pl.BlockSpec((tm, LANES), lambda j, i: (i, 0))

    def kern(x_ref, w_ref, c_ref, s1_ref, s2_ref, o_ref, wb_ref):
        _cast_weight_once(w_ref, wb_ref)
        acc = jnp.dot(x_ref[...], wb_ref[...], preferred_element_type=F32)
        c, s1, s2 = c_ref[...], s1_ref[...], s2_ref[...]
        for g in range(hpt):
            o_ref[g] = _rope(acc[:, g * LANES:(g + 1) * LANES], c, s1, s2).astype(BF16)

    return pl.pallas_call(
        kern,
        out_shape=jax.ShapeDtypeStruct((nh, m, LANES), BF16),
        grid=(n // tn, m // tm),
        in_specs=[pl.BlockSpec((tm, k), lambda j, i: (i, 0)),
                  pl.BlockSpec((k, tn), lambda j, i: (0, j)),
                  tab_spec, tab_spec, tab_spec],
        out_specs=pl.BlockSpec((hpt, tm, LANES), lambda j, i: (j, i, 0)),
        scratch_shapes=[pltpu.VMEM((k, tn), BF16)],
        compiler_params=_cparams(2, _mm_vmem(tm, tn, k, 2, extra=6 * tm * LANES * 4)),
        name="index_q_proj",
    )(x, w, *tables)


def _swiglu_kernel(x_ref, wg_ref, wu_ref, o_ref, wgb_ref, wub_ref):
    _cast_weight_once(wg_ref, wgb_ref)
    _cast_weight_once(wu_ref, wub_ref)
    x = x_ref[...]
    g = jnp.dot(x, wgb_ref[...], preferred_element_type=F32)
    u = jnp.dot(x, wub_ref[...], preferred_element_type=F32)
    o_ref[...] = (g * jax.nn.sigmoid(g) * u).astype(o_ref.dtype)


def _swiglu_up(x, wg, wu):
    m, k = x.shape
    n = wg.shape[1]
    tm = _pick(m, (1024, 512, 256, 128))
    tn = _pick(n, (256, 128))
    return pl.pallas_call(
        _swiglu_kernel,
        out_shape=jax.ShapeDtypeStruct((m, n), BF16),
        grid=(n // tn, m // tm),
        in_specs=[pl.BlockSpec((tm, k), lambda j, i: (i, 0)),
                  pl.BlockSpec((k, tn), lambda j, i: (0, j)),
                  pl.BlockSpec((k, tn), lambda j, i: (0, j))],
        out_specs=pl.BlockSpec((tm, tn), lambda j, i: (i, j)),
        scratch_shapes=[pltpu.VMEM((k, tn), BF16), pltpu.VMEM((k, tn), BF16)],
        compiler_params=_cparams(2, _mm_vmem(tm, tn, k, 2, n_w=2)),
        name="swiglu_up",
    )(x, wg, wu)


def _mm_residual_kernel(x_ref, w_ref, r_ref, o_ref, wb_ref, *, scale):
    _cast_weight_once(w_ref, wb_ref)
    acc = jnp.dot(x_ref[...], wb_ref[...], preferred_element_type=F32)
    if scale != 1.0:
        acc = acc * scale
    o_ref[...] = r_ref[...] + acc


def _matmul_residual(x, w, res, scale):
    m, k = x.shape
    n = w.shape[1]
    tm = _pick(m, (512, 256, 128))
    tn = _pick(n, (512, 256, 128))
    return pl.pallas_call(
        functools.partial(_mm_residual_kernel, scale=scale),
        out_shape=jax.ShapeDtypeStruct((m, n), F32),
        grid=(n // tn, m // tm),
        in_specs=[pl.BlockSpec((tm, k), lambda j, i: (i, 0)),
                  pl.BlockSpec((k, tn), lambda j, i: (0, j)),
                  pl.BlockSpec((tm, tn), lambda j, i: (i, j))],
        out_specs=pl.BlockSpec((tm, tn), lambda j, i: (i, j)),
        scratch_shapes=[pltpu.VMEM((k, tn), BF16)],
        compiler_params=_cparams(2, _mm_vmem(tm, tn, k, 4, extra=2 * tm * tn * 4)),
        name="matmul_residual",
    )(x, w, res)


ROW_BLOCK = 128
LOG2E = math.log2(math.e)
Q_PRESCALE = HEAD_DIM ** -0.5 * LOG2E


def _flash_init(m_ref, l_ref, acc_ref):
    m_ref[...] = jnp.full(m_ref.shape, NEG_INF, F32)
    l_ref[...] = jnp.zeros(l_ref.shape, F32)
    acc_ref[...] = jnp.zeros(acc_ref.shape, F32)


def _qk(q, k):
    return lax.dot_general(q, k, (((1,), (1,)), ((), ())), preferred_element_type=F32)


def _softmax_update(s, m_ref):
    m_prev = m_ref[...]
    m_new = jnp.maximum(m_prev, jnp.max(s, axis=-1, keepdims=True))
    m_safe = jnp.where(m_new == NEG_INF, 0.0, m_new)
    alpha = jnp.exp2(m_prev - m_safe)
    p = jnp.exp2(s - pltpu.repeat(m_safe, s.shape[1] // LANES, axis=1))
    m_ref[...] = m_new
    return p, alpha


def _attend(q, k, v1, bias, m_ref, l_ref, acc_ref):
    s = _qk(q, k)
    if bias is not None:
        s = s + bias
    p, alpha = _softmax_update(s, m_ref)
    pv = jnp.dot(p.astype(BF16), v1, preferred_element_type=F32)
    dv = acc_ref.shape[-1]
    l_ref[...] = alpha * l_ref[...] + pv[:, dv:]
    acc_ref[...] = alpha * acc_ref[...] + pv[:, :dv]


def _row_blocks(t):
    r = min(ROW_BLOCK, t)
    return [(rb, slice(rb * r, (rb + 1) * r), r) for rb in range(t // r)]


def _heads_per_step(n_heads, preferred):
    for c in (preferred, 2, 1):
        if c <= preferred and n_heads % c == 0:
            return c
    return 1


def _diff_attn_kernel(lam_ref, g_ref, q_ref, k_ref, v_ref, o_ref, m_ref, l_ref, acc_ref,
                      *, t, hp, lam_init):
    i = pl.program_id(2)
    dh = 2 * HEAD_DIM
    _flash_init(m_ref, l_ref, acc_ref)

    def step(off, width, rows, r, g, bias):
        v = v_ref[pl.ds(off, width), g * dh:(g + 1) * dh]
        ps, alphas = [], []
        for c in range(2):
            cols = slice(g * dh + c * HEAD_DIM, g * dh + (c + 1) * HEAD_DIM)
            s = _qk(q_ref[rows, cols], k_ref[pl.ds(off, width), cols])
            if bias is not None:
                s = s + bias
            slot = 2 * g + c
            p, alpha = _softmax_update(s, m_ref.at[slot, rows])
            l_ref[slot, rows] = alpha * l_ref[slot, rows] + jnp.sum(p, axis=-1, keepdims=True)
            ps.append(p.astype(BF16))
            alphas.append(alpha)
        pv = jnp.dot(jnp.concatenate(ps, axis=0), v, preferred_element_type=F32)
        for c in range(2):
            slot = 2 * g + c
            acc_ref[slot, rows] = (pltpu.repeat(alphas[c], dh // LANES, axis=1) * acc_ref[slot, rows]
                                   + pv[c * r:(c + 1) * r])

    diag_off = pl.multiple_of(i * t, t)
    for rb, rows, r in _row_blocks(t):
        width = (rb + 1) * r
        row = rb * r + lax.broadcasted_iota(I32, (r, width), 0)
        col = lax.broadcasted_iota(I32, (r, width), 1)
        bias = jnp.where(col <= row, 0.0, NEG_INF)
        for g in range(hp):
            step(diag_off, width, rows, r, g, bias)

    def body(j, carry):
        off = pl.multiple_of(j * t, t)
        for rb, rows, r in _row_blocks(t):
            for g in range(hp):
                step(off, t, rows, r, g, None)
        return carry

    lax.fori_loop(0, i, body, 0)

    p = lam_ref[...]
    lam = (jnp.exp(jnp.sum(p[0:1] * p[1:2], axis=-1, keepdims=True))
           - jnp.exp(jnp.sum(p[2:3] * p[3:4], axis=-1, keepdims=True)) + lam_init)
    for g in range(hp):
        w1 = pltpu.repeat(1.0 / l_ref[2 * g], dh // LANES, axis=1)
        w2 = pltpu.repeat(lam / l_ref[2 * g + 1], dh // LANES, axis=1)
        o = acc_ref[2 * g] * w1 - acc_ref[2 * g + 1] * w2
        y = o * lax.rsqrt(jnp.mean(o * o, axis=-1, keepdims=True) + SUBLN_EPS) * g_ref[...]
        o_ref[:, g * dh:(g + 1) * dh] = (y * (1.0 - lam_init)).astype(o_ref.dtype)


def _diff_attention(qkv, lam_params, subln_g, lam_init, batch, seq):
    m, three_d = qkv.shape
    d = three_d // 3
    dh = 2 * HEAD_DIM
    nh = d // dh
    hp = _heads_per_step(nh, 2)
    ng = nh // hp
    w = hp * dh
    t = _pick(seq, (512, 256, 128))
    nq = seq // t
    vmem = (2 * (2 * seq * w * 2) + 4 * t * w * 2 + 2 * hp * t * (2 * LANES + dh) * 4
            + 16 * ROW_BLOCK * t * 4 + 8 * MIB)
    kern = functools.partial(_diff_attn_kernel, t=t, hp=hp, lam_init=lam_init)
    return pl.pallas_call(
        kern,
        out_shape=jax.ShapeDtypeStruct((m, d), BF16),
        grid=(batch, ng, nq),
        in_specs=[pl.BlockSpec((4, HEAD_DIM), lambda b, h, i: (0, 0)),
                  pl.BlockSpec((1, dh), lambda b, h, i: (0, 0)),
                  pl.BlockSpec((t, w), lambda b, h, i: (b * nq + i, h)),
                  pl.BlockSpec((seq, w), lambda b, h, i: (b, ng + h)),
                  pl.BlockSpec((seq, w), lambda b, h, i: (b, 2 * ng + h))],
        out_specs=pl.BlockSpec((t, w), lambda b, h, i: (b * nq + i, h)),
        scratch_shapes=[pltpu.VMEM((2 * hp, t, LANES), F32), pltpu.VMEM((2 * hp, t, LANES), F32),
                        pltpu.VMEM((2 * hp, t, dh), F32)],
        compiler_params=_cparams(3, vmem),
        name="diff_attention",
    )(lam_params, subln_g.reshape(1, dh), qkv, qkv, qkv)


def _moba_attn_kernel(q_ref, k_ref, v_ref, o_ref, kmean_ref, selb_ref, m_ref, l_ref, acc_ref,
                      *, t, hp, blk, nbp, topk):
    i = pl.program_id(2)
    seq = k_ref.shape[0]
    bpc = t // blk
    blk_shift = blk.bit_length() - 1
    heads = [slice(g * HEAD_DIM, (g + 1) * HEAD_DIM) for g in range(hp)]

    @pl.when(i == 0)
    def _():
        r = lax.broadcasted_iota(I32, (nbp, seq), 0)
        c = lax.broadcasted_iota(I32, (nbp, seq), 1)
        avg = jnp.where(c >= r * blk, jnp.where(c < (r + 1) * blk, 1.0 / blk, 0.0), 0.0).astype(BF16)
        for g, sl in enumerate(heads):
            kmean_ref[g] = jnp.dot(avg, k_ref[:, sl], preferred_element_type=F32).astype(BF16)

    n_idx = lax.broadcasted_iota(I32, (t, nbp), 1)
    n_f = n_idx.astype(F32)
    qblk = lax.shift_right_logical(i * t + lax.broadcasted_iota(I32, (t, nbp), 0), blk_shift)
    valid = n_idx < qblk
    for g, sl in enumerate(heads):
        gate = _qk(q_ref[:, sl], kmean_ref[g])
        gt = jnp.where(valid, gate, NEG_INF)
        picked = jnp.zeros((t, nbp), F32)
        for _ in range(topk):
            mx = jnp.max(gt, axis=-1, keepdims=True)
            first = jnp.min(jnp.where(gt == mx, n_f, float(nbp)), axis=-1, keepdims=True)
            hit = n_f == first
            picked = jnp.where(hit, 1.0, picked)
            gt = jnp.where(hit, NEG_INF, gt)
        selb_ref[g] = jnp.where(valid, jnp.where(picked > 0.5, 0.0, NEG_INF), NEG_INF)

    _flash_init(m_ref, l_ref, acc_ref)
    ones = jnp.ones((t, LANES), BF16)

    def block_bias(g, rows, r, j, width):
        sb = selb_ref[g, rows, :]
        nr = lax.broadcasted_iota(I32, (r, nbp), 1)
        col_blk = lax.shift_right_logical(lax.broadcasted_iota(I32, (r, width), 1), blk_shift)
        out = None
        for b in range(-(-width // blk)):
            col = jnp.max(jnp.where(nr == j * bpc + b, sb, NEG_INF), axis=-1, keepdims=True)
            out = col if out is None else jnp.where(col_blk >= b, col, out)
        return out

    diag_off = pl.multiple_of(i * t, t)
    for rb, rows, r in _row_blocks(t):
        width = (rb + 1) * r
        row = rb * r + lax.broadcasted_iota(I32, (r, width), 0)
        col = lax.broadcasted_iota(I32, (r, width), 1)
        own = lax.shift_right_logical(row, blk_shift) == lax.shift_right_logical(col, blk_shift)
        for g, sl in enumerate(heads):
            bias = jnp.where(col <= row, jnp.where(own, 0.0, block_bias(g, rows, r, i, width)), NEG_INF)
            v1 = jnp.concatenate([v_ref[pl.ds(diag_off, width), sl], ones[:width]], axis=1)
            _attend(q_ref[rows, sl], k_ref[pl.ds(diag_off, width), sl], v1, bias,
                    m_ref.at[g, rows], l_ref.at[g, rows], acc_ref.at[g, rows])

    def body(j, carry):
        off = pl.multiple_of(j * t, t)
        ks = [k_ref[pl.ds(off, t), sl] for sl in heads]
        v1s = [jnp.concatenate([v_ref[pl.ds(off, t), sl], ones], axis=1) for sl in heads]
        for rb, rows, r in _row_blocks(t):
            for g, sl in enumerate(heads):
                _attend(q_ref[rows, sl], ks[g], v1s[g], block_bias(g, rows, r, j, t),
                        m_ref.at[g, rows], l_ref.at[g, rows], acc_ref.at[g, rows])
        return carry

    lax.fori_loop(0, i, body, 0)
    for g, sl in enumerate(heads):
        o_ref[:, sl] = (acc_ref[g] * (1.0 / l_ref[g])).astype(o_ref.dtype)


def _moba_attention(qkv, batch, seq):
    m, three_d = qkv.shape
    d = three_d // 3
    nh = d // HEAD_DIM
    assert seq % MOBA_BLOCK == 0 and MOBA_BLOCK & (MOBA_BLOCK - 1) == 0
    t = _pick(seq, (512, 256))
    assert t % MOBA_BLOCK == 0 and MOBA_BLOCK % ROW_BLOCK == 0
    hp = _heads_per_step(nh, 4)
    ng = nh // hp
    w = hp * HEAD_DIM
    nq = seq // t
    nb = seq // MOBA_BLOCK
    nbp = -(-nb // 16) * 16
    vmem = (2 * (2 * seq * w * 2) + 4 * t * w * 2 + hp * t * 4 * LANES * 4
            + 16 * ROW_BLOCK * t * 4 + 8 * MIB)
    kern = functools.partial(_moba_attn_kernel, t=t, hp=hp, blk=MOBA_BLOCK, nbp=nbp,
                             topk=min(MOBA_TOPK, nb))
    return pl.pallas_call(
        kern,
        out_shape=jax.ShapeDtypeStruct((m, d), BF16),
        grid=(batch, ng, nq),
        in_specs=[pl.BlockSpec((t, w), lambda b, h, i: (b * nq + i, h)),
                  pl.BlockSpec((seq, w), lambda b, h, i: (b, ng + h)),
                  pl.BlockSpec((seq, w), lambda b, h, i: (b, 2 * ng + h))],
        out_specs=pl.BlockSpec((t, w), lambda b, h, i: (b * nq + i, h)),
        scratch_shapes=[pltpu.VMEM((hp, nbp, HEAD_DIM), BF16), pltpu.VMEM((hp, t, nbp), F32),
                        pltpu.VMEM((hp, t, LANES), F32), pltpu.VMEM((hp, t, LANES), F32),
                        pltpu.VMEM((hp, t, HEAD_DIM), F32)],
        compiler_params=_cparams(3, vmem),
        name="moba_attention",
    )(qkv, qkv, qkv)


def _dsa_index_kernel(qi_ref, ki_ref, w_ref, bias_ref, key_ref, wb_ref, *, tq, tk, nh, topk):
    i = pl.program_id(1)
    w = w_ref[...]
    for h in range(nh):
        wb_ref[h] = jnp.broadcast_to(w[:, h:h + 1], (tq, LANES))
    key_ref[...] = jnp.full(key_ref.shape, INT_MIN, I32)

    q_all = qi_ref[...].reshape(nh * tq, HEAD_DIM)
    qpos = i * tq + lax.broadcasted_iota(I32, (tq, tk), 0)
    col = lax.broadcasted_iota(I32, (tq, tk), 1)
    n_chunks = (i * tq + tq + tk - 1) // tk

    def score_chunk(c, carry):
        off = pl.multiple_of(c * tk, tk)
        kc = ki_ref[pl.ds(off, tk), :]
        r = _qk(q_all, kc)
        acc = jnp.zeros((tq, tk), F32)
        for h in range(nh):
            wbh = wb_ref[h]
            wfull = jnp.concatenate([wbh] * (tk // LANES), axis=1)
            acc = acc + jnp.maximum(r[h * tq:(h + 1) * tq], 0.0) * wfull
        bits = lax.bitcast_convert_type(acc, I32)
        key = bits ^ (lax.shift_right_arithmetic(bits, 31) & 0x7FFFFFFF)
        key_ref[c] = jnp.where(off + col <= qpos, key, INT_MIN)
        return carry

    lax.fori_loop(0, n_chunks, score_chunk, 0)

    def count_ge(cand):
        ge = jnp.where(key_ref[...] >= cand, 1.0, 0.0)
        return jnp.sum(jnp.sum(ge, axis=0), axis=-1, keepdims=True)

    zero = jnp.zeros((tq, 1), I32)
    thr = jnp.where(count_ge(zero) >= topk, zero, INT_MIN)

    def bisect(b, thr):
        cand = thr + lax.shift_left(jnp.int32(1), 30 - b)
        return jnp.where(count_ge(cand) >= topk, cand, thr)

    thr = lax.fori_loop(0, 31, bisect, thr)
    thr = jnp.maximum(thr, INT_MIN + 1)
    bias_ref[...] = jnp.where(key_ref[...] >= thr, 0.0, NEG_INF).astype(bias_ref.dtype)


def _dsa_index_bias(qi, ki, wts, batch, seq, tk):
    nh, m, _ = qi.shape
    tq = _pick(seq, (128,))
    nq = seq // tq
    nkc = seq // tk
    topk = min(DSA_TOPK_MAX, seq // 4)
    vmem = (2 * nh * tq * HEAD_DIM * 2 + 2 * seq * HEAD_DIM * 2 + 2 * tq * LANES * 4
            + 2 * tq * seq * 2 + tq * seq * 4 + nh * tq * LANES * 4
            + 2 * nh * tq * tk * 4 + 4 * tq * seq * 4 + 6 * MIB)
    kern = functools.partial(_dsa_index_kernel, tq=tq, tk=tk, nh=nh, topk=topk)
    return pl.pallas_call(
        kern,
        out_shape=jax.ShapeDtypeStruct((batch, nkc, seq, tk), BF16),
        grid=(batch, nq),
        in_specs=[pl.BlockSpec((nh, tq, HEAD_DIM), lambda b, i: (0, b * nq + i, 0)),
                  pl.BlockSpec((seq, HEAD_DIM), lambda b, i: (b, 0)),
                  pl.BlockSpec((tq, LANES), lambda b, i: (b * nq + i, 0))],
        out_specs=pl.BlockSpec((None, nkc, tq, tk), lambda b, i: (b, 0, i, 0)),
        scratch_shapes=[pltpu.VMEM((nkc, tq, tk), I32), pltpu.VMEM((nh, tq, LANES), F32)],
        compiler_params=_cparams(2, vmem),
        name="dsa_index_bias",
    )(qi, ki, wts)


def _dsa_attn_kernel(q_ref, k_ref, v_ref, bias_ref, o_ref, m_ref, l_ref, acc_ref, *, t, hp):
    i = pl.program_id(1)
    _flash_init(m_ref, l_ref, acc_ref)
    ones = jnp.ones((t, LANES), BF16)
    heads = [slice(g * HEAD_DIM, (g + 1) * HEAD_DIM) for g in range(hp)]

    diag_off = pl.multiple_of(i * t, t)
    for rb, rows, r in _row_blocks(t):
        width = (rb + 1) * r
        bias = bias_ref[i, rows, :width].astype(F32)
        for g, sl in enumerate(heads):
            v1 = jnp.concatenate([v_ref[pl.ds(diag_off, width), sl], ones[:width]], axis=1)
            _attend(q_ref[rows, sl], k_ref[pl.ds(diag_off, width), sl], v1, bias,
                    m_ref.at[g, rows], l_ref.at[g, rows], acc_ref.at[g, rows])

    def body(j, carry):
        off = pl.multiple_of(j * t, t)
        ks = [k_ref[pl.ds(off, t), sl] for sl in heads]
        v1s = [jnp.concatenate([v_ref[pl.ds(off, t), sl], ones], axis=1) for sl in heads]
        for rb, rows, r in _row_blocks(t):
            bias = bias_ref[j, rows, :].astype(F32)
            for g, sl in enumerate(heads):
                _attend(q_ref[rows, sl], ks[g], v1s[g], bias,
                        m_ref.at[g, rows], l_ref.at[g, rows], acc_ref.at[g, rows])
        return carry

    lax.fori_loop(0, i, body, 0)
    for g, sl in enumerate(heads):
        o_ref[:, sl] = (acc_ref[g] * (1.0 / l_ref[g])).astype(o_ref.dtype)


def _dsa_attention(qkv, bias, batch, seq, t):
    m, three_d = qkv.shape
    d = three_d // 3
    nh = d // HEAD_DIM
    hp = _heads_per_step(nh, 4)
    ng = nh // hp
    w = hp * HEAD_DIM
    nq = seq // t
    nkc = seq // t
    vmem = (2 * (2 * seq * w * 2) + 4 * t * w * 2 + 2 * nkc * t * t * 2
            + hp * t * 3 * LANES * 4 + 16 * ROW_BLOCK * t * 4 + 8 * MIB)
    kern = functools.partial(_dsa_attn_kernel, t=t, hp=hp)
    return pl.pallas_call(
        kern,
        out_shape=jax.ShapeDtypeStruct((m, d), BF16),
        grid=(batch, nq, ng),
        in_specs=[pl.BlockSpec((t, w), lambda b, i, h: (b * nq + i, h)),
                  pl.BlockSpec((seq, w), lambda b, i, h: (b, ng + h)),
                  pl.BlockSpec((seq, w), lambda b, i, h: (b, 2 * ng + h)),
                  pl.BlockSpec((None, nkc, t, t), lambda b, i, h: (b, 0, i, 0))],
        out_specs=pl.BlockSpec((t, w), lambda b, i, h: (b * nq + i, h)),
        scratch_shapes=[pltpu.VMEM((hp, t, LANES), F32), pltpu.VMEM((hp, t, LANES), F32),
                        pltpu.VMEM((hp, t, HEAD_DIM), F32)],
        compiler_params=_cparams(3, vmem),
        name="dsa_attention",
    )(qkv, qkv, qkv, bias)


def _lambda_init_for(layer):
    return 0.8 - 0.6 * math.exp(-0.3 * layer)


def _ffn(h, g, w_gate, w_up, w_down):
    act = _swiglu_up(_rmsnorm(h, g, BF16), w_gate, w_up)
    return _matmul_residual(act, w_down, h, 0.5)


def kernel(x, positions, norm_g, ffn_w_gate, ffn_w_up, ffn_w_down, attn_w_in, attn_w_out,
           diff_lambda_q1, diff_lambda_k1, diff_lambda_q2, diff_lambda_k2, diff_subln_g,
           idx_w_q, idx_w_k, idx_w_head, final_norm_g):
    batch, seq, d = x.shape
    depth = norm_g.shape[0]
    m = batch * seq
    tables = _rope_tables(positions)
    h = x.reshape(m, d)
    for i in range(depth):
        h = _ffn(h, norm_g[i, 0], ffn_w_gate[i, 0], ffn_w_up[i, 0], ffn_w_down[i, 0])
        hn = _rmsnorm(h, norm_g[i, 1], BF16)
        qkv = _matmul_rope(hn, attn_w_in[i], tables, d, 2 * d, Q_PRESCALE)
        mixer = i % N_MIXERS
        j = i // N_MIXERS
        if mixer == 0:
            lam_params = jnp.stack([diff_lambda_q1[j], diff_lambda_k1[j],
                                    diff_lambda_q2[j], diff_lambda_k2[j]])
            mix = _diff_attention(qkv, lam_params, diff_subln_g[j], _lambda_init_for(i), batch, seq)
        elif mixer == 1:
            mix = _moba_attention(qkv, batch, seq)
        else:
            n_idx_heads = idx_w_head.shape[-1]
            idx_dim = idx_w_k.shape[-1]
            qi = _index_q_proj(hn, idx_w_q[j], tables)
            ki, wts = _index_kw_proj(hn, idx_w_k[j], idx_w_head[j], tables,
                                     n_idx_heads ** -0.5 * idx_dim ** -0.5)
            t = _pick(seq, (512, 256))
            bias = _dsa_index_bias(qi, ki, wts, batch, seq, t)
            mix = _dsa_attention(qkv, bias, batch, seq, t)
        h = _matmul_residual(mix, attn_w_out[i], h, 1.0)
        h = _ffn(h, norm_g[i, 2], ffn_w_gate[i, 1], ffn_w_up[i, 1], ffn_w_down[i, 1])
    return _rmsnorm(h, final_norm_g, x.dtype).reshape(batch, seq, d)
```

```python
import functools
import math

import jax
import jax.numpy as jnp
from jax import lax
from jax.experimental import pallas as pl
from jax.experimental.pallas import tpu as pltpu

F32 = jnp.float32
BF16 = jnp.bfloat16
I32 = jnp.int32

HEAD_DIM = 128
ROT_DIM = HEAD_DIM // 4
ROT_HALF = ROT_DIM // 2
ROPE_THETA = 500000.0
RMS_EPS = 1e-6
SUBLN_EPS = 1e-5
MOBA_BLOCK = 256
MOBA_TOPK = 3
DSA_TOPK_MAX = 256
N_MIXERS = 3

LANES = 128
MIB = 1024 * 1024
VMEM_CAP_BYTES = 60 * MIB
NEG_INF = float("-inf")
INT_MIN = -(2 ** 31)


def _cparams(n_axes, vmem_bytes):
    return pltpu.CompilerParams(
        dimension_semantics=("arbitrary",) * n_axes,
        vmem_limit_bytes=int(min(VMEM_CAP_BYTES, vmem_bytes)))


def _pick(n, candidates):
    for c in candidates:
        if n % c == 0:
            return c
    raise ValueError(f"no tile in {candidates} divides {n}")


def _rmsnorm_kernel(x_ref, g_ref, o_ref, *, eps):
    x = x_ref[...]
    y = x * lax.rsqrt(jnp.mean(x * x, axis=-1, keepdims=True) + eps) * g_ref[...]
    o_ref[...] = y.astype(o_ref.dtype)


def _rmsnorm(x, g, out_dtype):
    m, d = x.shape
    tm = _pick(m, (256, 128, 8))
    vmem = 2 * tm * d * (4 + jnp.dtype(out_dtype).itemsize) + 4 * tm * d * 4 + 4 * MIB
    return pl.pallas_call(
        functools.partial(_rmsnorm_kernel, eps=RMS_EPS),
        out_shape=jax.ShapeDtypeStruct((m, d), out_dtype),
        grid=(m // tm,),
        in_specs=[pl.BlockSpec((tm, d), lambda i: (i, 0)),
                  pl.BlockSpec((1, d), lambda i: (0, 0))],
        out_specs=pl.BlockSpec((tm, d), lambda i: (i, 0)),
        compiler_params=_cparams(1, vmem),
        name="rmsnorm",
    )(x, g.reshape(1, d))


def _rope_table_kernel(pos_ref, inv_ref, c_ref, s1_ref, s2_ref):
    ang = pos_ref[...].astype(F32) * inv_ref[...]
    lane = lax.broadcasted_iota(I32, ang.shape, 1)
    cos = jnp.cos(ang)
    sin = jnp.sin(ang)
    c_ref[...] = jnp.where(lane < ROT_DIM, cos, 1.0)
    s1_ref[...] = jnp.where(lane < ROT_HALF, 0.0, jnp.where(lane < ROT_DIM, sin, 0.0))
    s2_ref[...] = jnp.where(lane < ROT_HALF, -sin, 0.0)


def _rope_tables(positions):
    m = positions.size
    inv = ROPE_THETA ** (-jnp.arange(0, ROT_DIM, 2, dtype=F32) / ROT_DIM)
    inv = jnp.concatenate([inv, inv, jnp.zeros((LANES - ROT_DIM,), F32)]).reshape(1, LANES)
    tm = _pick(m, (512, 256, 128, 8))
    spec = pl.BlockSpec((tm, LANES), lambda i: (i, 0))
    return pl.pallas_call(
        _rope_table_kernel,
        out_shape=[jax.ShapeDtypeStruct((m, LANES), F32)] * 3,
        grid=(m // tm,),
        in_specs=[pl.BlockSpec((tm, 1), lambda i: (i, 0)),
                  pl.BlockSpec((1, LANES), lambda i: (0, 0))],
        out_specs=[spec, spec, spec],
        compiler_params=_cparams(1, 16 * MIB),
        name="rope_tables",
    )(positions.reshape(m, 1).astype(I32), inv)


def _rope(a, c, s1, s2):
    return (a * c + pltpu.roll(a, ROT_HALF, 1) * s1
            + pltpu.roll(a, LANES - ROT_HALF, 1) * s2)


def _mm_vmem(tm, tn, k, out_bytes, n_w=1, extra=0):
    return (2 * tm * k * 2 + n_w * k * tn * (2 * 4 + 2) + 2 * tm * tn * out_bytes
            + (n_w + 1) * tm * tn * 4 + extra + 4 * MIB)


def _weight_spec(w, widx, k, tn):
    assert w.shape[len(widx):-1] == (k,) and len(widx) == w.ndim - 2
    return pl.BlockSpec((None,) * len(widx) + (k, tn), lambda j, i: tuple(widx) + (0, j))


def _cast_weight_once(w_ref, wb_ref):
    @pl.when(pl.program_id(1) == 0)
    def _():
        wb_ref[...] = w_ref[...].astype(BF16)


def _mm_rope_kernel(x_ref, w_ref, c_ref, s1_ref, s2_ref, o_ref, wb_ref,
                    *, tn, n_q_tiles, n_rope_tiles, q_scale):
    j = pl.program_id(0)
    _cast_weight_once(w_ref, wb_ref)

    def matmul():
        return jnp.dot(x_ref[...], wb_ref[...], preferred_element_type=F32)

    def roped(scale):
        acc = matmul()
        c, s1, s2 = c_ref[...], s1_ref[...], s2_ref[...]
        for g in range(tn // LANES):
            sl = slice(g * LANES, (g + 1) * LANES)
            y = _rope(acc[:, sl], c, s1, s2)
            o_ref[:, sl] = (y if scale is None else y * scale).astype(o_ref.dtype)

    @pl.when(j < n_q_tiles)
    def _():
        roped(q_scale)

    @pl.when((j >= n_q_tiles) & (j < n_rope_tiles))
    def _():
        roped(None)

    @pl.when(j >= n_rope_tiles)
    def _():
        o_ref[...] = matmul().astype(o_ref.dtype)


def _matmul_rope(x, w, widx, tables, n_q_cols, n_rope_cols, q_scale, out_dtype=BF16):
    m, k = x.shape
    n = w.shape[-1]
    tm = _pick(m, (512, 256, 128))
    tn = _pick(n, (1024, 512, 256, 128))
    assert n_rope_cols % tn == 0 and n_q_cols % tn == 0 and n_q_cols <= n_rope_cols
    tab_spec = pl.BlockSpec((tm, LANES), lambda j, i: (i, 0))
    kern = functools.partial(_mm_rope_kernel, tn=tn, n_q_tiles=n_q_cols // tn,
                             n_rope_tiles=n_rope_cols // tn, q_scale=q_scale)
    return pl.pallas_call(
        kern,
        out_shape=jax.ShapeDtypeStruct((m, n), out_dtype),
        grid=(n // tn, m // tm),
        in_specs=[pl.BlockSpec((tm, k), lambda j, i: (i, 0)),
                  _weight_spec(w, widx, k, tn),
                  tab_spec, tab_spec, tab_spec],
        out_specs=pl.BlockSpec((tm, tn), lambda j, i: (i, j)),
        scratch_shapes=[pltpu.VMEM((k, tn), BF16)],
        compiler_params=_cparams(2, _mm_vmem(tm, tn, k, 2, extra=6 * tm * LANES * 4)),
        name="matmul_rope",
    )(x, w, *tables)


def _index_kw_proj(x, w_k, w_head, tables, head_scale):
    m, k = x.shape
    nh = w_head.shape[1]
    assert w_k.shape[1] == LANES and nh <= LANES
    w = jnp.concatenate([w_k, w_head, jnp.zeros((k, LANES - nh), w_k.dtype)], axis=1)
    tm = _pick(m, (1024, 512, 256, 128))
    tab_spec = pl.BlockSpec((tm, LANES), lambda j, i: (i, 0))
    out_spec = pl.BlockSpec((tm, LANES), lambda j, i: (i, 0))

    def kern(x_ref, w_ref, c_ref, s1_ref, s2_ref, ki_ref, wt_ref, wb_ref):
        _cast_weight_once(w_ref, wb_ref)
        acc = jnp.dot(x_ref[...], wb_ref[...], preferred_element_type=F32)
        ki_ref[...] = _rope(acc[:, :LANES], c_ref[...], s1_ref[...], s2_ref[...]).astype(BF16)
        wt_ref[...] = acc[:, LANES:] * head_scale

    return pl.pallas_call(
        kern,
        out_shape=[jax.ShapeDtypeStruct((m, LANES), BF16), jax.ShapeDtypeStruct((m, LANES), F32)],
        grid=(1, m // tm),
        in_specs=[pl.BlockSpec((tm, k), lambda j, i: (i, 0)),
                  pl.BlockSpec((k, 2 * LANES), lambda j, i: (0, 0)),
                  tab_spec, tab_spec, tab_spec],
        out_specs=[out_spec, out_spec],
        scratch_shapes=[pltpu.VMEM((k, 2 * LANES), BF16)],
        compiler_params=_cparams(2, _mm_vmem(tm, 2 * LANES, k, 4, extra=6 * tm * LANES * 4)),
        name="index_kw_proj",
    )(x, w, *tables)


def _index_q_proj(x, w, widx, tables):
    m, k = x.shape
    n = w.shape[-1]
    nh = n // LANES
    tm = _pick(m, (512, 256, 128))
    tn = _pick(n, (1024, 512, 256, 128))
    hpt = tn // LANES
    tab_spec = pl.BlockSpec((tm, LANES), lambda j, i: (i, 0))

    def kern(x_ref, w_ref, c_ref, s1_ref, s2_ref, o_ref, wb_ref):
        _cast_weight_once(w_ref, wb_ref)
        acc = jnp.dot(x_ref[...], wb_ref[...], preferred_element_type=F32)
        c, s1, s2 = c_ref[...], s1_ref[...], s2_ref[...]
        for g in range(hpt):
            o_ref[g] = _rope(acc[:, g * LANES:(g + 1) * LANES], c, s1, s2).astype(BF16)

    return pl.pallas_call(
        kern,
        out_shape=jax.ShapeDtypeStruct((nh, m, LANES), BF16),
        grid=(n // tn, m // tm),
        in_specs=[pl.BlockSpec((tm, k), lambda j, i: (i, 0)),
                  _weight_spec(w, widx, k, tn),
                  tab_spec, tab_spec, tab_spec],
        out_specs=pl.BlockSpec((hpt, tm, LANES), lambda j, i: (j, i, 0)),
        scratch_shapes=[pltpu.VMEM((k, tn), BF16)],
        compiler_params=_cparams(2, _mm_vmem(tm, tn, k, 2, extra=6 * tm * LANES * 4)),
        name="index_q_proj",
    )(x, w, *tables)


def _swiglu_kernel(x_ref, wg_ref, wu_ref, o_ref, wgb_ref, wub_ref):
    _cast_weight_once(wg_ref, wgb_ref)
    _cast_weight_once(wu_ref, wub_ref)
    x = x_ref[...]
    g = jnp.dot(x, wgb_ref[...], preferred_element_type=F32)
    u = jnp.dot(x, wub_ref[...], preferred_element_type=F32)
    o_ref[...] = (g * jax.nn.sigmoid(g) * u).astype(o_ref.dtype)


def _swiglu_up(x, wg, wu, widx):
    m, k = x.shape
    n = wg.shape[-1]
    tm = _pick(m, (512, 256, 128))
    tn = _pick(n, (512, 256, 128))
    return pl.pallas_call(
        _swiglu_kernel,
        out_shape=jax.ShapeDtypeStruct((m, n), BF16),
        grid=(n // tn, m // tm),
        in_specs=[pl.BlockSpec((tm, k), lambda j, i: (i, 0)),
                  _weight_spec(wg, widx, k, tn),
                  _weight_spec(wu, widx, k, tn)],
        out_specs=pl.BlockSpec((tm, tn), lambda j, i: (i, j)),
        scratch_shapes=[pltpu.VMEM((k, tn), BF16), pltpu.VMEM((k, tn), BF16)],
        compiler_params=_cparams(2, _mm_vmem(tm, tn, k, 2, n_w=2)),
        name="swiglu_up",
    )(x, wg, wu)


def _mm_residual_kernel(x_ref, w_ref, r_ref, o_ref, wb_ref, *, scale):
    _cast_weight_once(w_ref, wb_ref)
    acc = jnp.dot(x_ref[...], wb_ref[...], preferred_element_type=F32)
    if scale != 1.0:
        acc = acc * scale
    o_ref[...] = r_ref[...] + acc


def _matmul_residual(x, w, widx, res, scale, tn_pref):
    m, k = x.shape
    n = w.shape[-1]
    tn = _pick(n, tuple(c for c in (1024, 512, 256, 128) if c <= tn_pref))
    tm = _pick(m, (512, 256, 128) if tn <= 512 else (256, 128))
    return pl.pallas_call(
        functools.partial(_mm_residual_kernel, scale=scale),
        out_shape=jax.ShapeDtypeStruct((m, n), F32),
        grid=(n // tn, m // tm),
        in_specs=[pl.BlockSpec((tm, k), lambda j, i: (i, 0)),
                  _weight_spec(w, widx, k, tn),
                  pl.BlockSpec((tm, tn), lambda j, i: (i, j))],
        out_specs=pl.BlockSpec((tm, tn), lambda j, i: (i, j)),
        scratch_shapes=[pltpu.VMEM((k, tn), BF16)],
        compiler_params=_cparams(2, _mm_vmem(tm, tn, k, 4, extra=2 * tm * tn * 4)),
        name="matmul_residual",
    )(x, w, res)


ROW_BLOCK = 128
LOG2E = math.log2(math.e)
Q_PRESCALE = HEAD_DIM ** -0.5 * LOG2E


def _flash_init(m_ref, l_ref, acc_ref):
    m_ref[...] = jnp.full(m_ref.shape, NEG_INF, F32)
    l_ref[...] = jnp.zeros(l_ref.shape, F32)
    acc_ref[...] = jnp.zeros(acc_ref.shape, F32)


def _qk(q, k):
    return lax.dot_general(q, k, (((1,), (1,)), ((), ())), preferred_element_type=F32)


def _lane_tile(x, n):
    return x if n == 1 else jnp.concatenate([x] * n, axis=1)


def _softmax_update(s, m_ref):
    m_prev = m_ref[...]
    m_new = jnp.maximum(m_prev, jnp.max(s, axis=-1, keepdims=True))
    m_safe = jnp.where(m_new == NEG_INF, 0.0, m_new)
    alpha = jnp.exp2(m_prev - m_safe)
    p = jnp.exp2(s - _lane_tile(m_safe, s.shape[1] // LANES))
    m_ref[...] = m_new
    return p, alpha


def _attend(q, k, v1, bias, m_ref, l_ref, acc_ref):
    s = _qk(q, k)
    if bias is not None:
        s = s + bias
    p, alpha = _softmax_update(s, m_ref)
    pv = jnp.dot(p.astype(BF16), v1, preferred_element_type=F32)
    dv = acc_ref.shape[-1]
    l_ref[...] = alpha * l_ref[...] + pv[:, dv:]
    acc_ref[...] = alpha * acc_ref[...] + pv[:, :dv]


def _row_blocks(t):
    r = min(ROW_BLOCK, t)
    return [(rb, slice(rb * r, (rb + 1) * r), r) for rb in range(t // r)]


def _heads_per_step(n_heads, preferred):
    for c in (preferred, 2, 1):
        if c <= preferred and n_heads % c == 0:
            return c
    return 1


def _diff_attn_kernel(lam_ref, g_ref, q_ref, k_ref, v_ref, o_ref, m_ref, l_ref, acc_ref,
                      *, t, hp, lam_init):
    i = pl.program_id(2)
    dh = 2 * HEAD_DIM
    _flash_init(m_ref, l_ref, acc_ref)

    def step(off, width, rows, r, g, bias):
        v = v_ref[pl.ds(off, width), g * dh:(g + 1) * dh]
        ps, alphas = [], []
        for c in range(2):
            cols = slice(g * dh + c * HEAD_DIM, g * dh + (c + 1) * HEAD_DIM)
            s = _qk(q_ref[rows, cols], k_ref[pl.ds(off, width), cols])
            if bias is not None:
                s = s + bias
            slot = 2 * g + c
            p, alpha = _softmax_update(s, m_ref.at[slot, rows])
            l_ref[slot, rows] = alpha * l_ref[slot, rows] + jnp.sum(p, axis=-1, keepdims=True)
            ps.append(p.astype(BF16))
            alphas.append(alpha)
        pv = jnp.dot(jnp.concatenate(ps, axis=0), v, preferred_element_type=F32)
        for c in range(2):
            slot = 2 * g + c
            acc_ref[slot, rows] = (_lane_tile(alphas[c], dh // LANES) * acc_ref[slot, rows]
                                   + pv[c * r:(c + 1) * r])

    diag_off = pl.multiple_of(i * t, t)
    for rb, rows, r in _row_blocks(t):
        width = (rb + 1) * r
        row = rb * r + lax.broadcasted_iota(I32, (r, width), 0)
        col = lax.broadcasted_iota(I32, (r, width), 1)
        bias = jnp.where(col <= row, 0.0, NEG_INF)
        for g in range(hp):
            step(diag_off, width, rows, r, g, bias)

    def body(j, carry):
        off = pl.multiple_of(j * t, t)
        for rb, rows, r in _row_blocks(t):
            for g in range(hp):
                step(off, t, rows, r, g, None)
        return carry

    lax.fori_loop(0, i, body, 0)

    p = lam_ref[...]
    lam = (jnp.exp(jnp.sum(p[0:1] * p[1:2], axis=-1, keepdims=True))
           - jnp.exp(jnp.sum(p[2:3] * p[3:4], axis=-1, keepdims=True)) + lam_init)
    for g in range(hp):
        w1 = _lane_tile(1.0 / l_ref[2 * g], dh // LANES)
        w2 = _lane_tile(lam / l_ref[2 * g + 1], dh // LANES)
        o = acc_ref[2 * g] * w1 - acc_ref[2 * g + 1] * w2
        y = o * lax.rsqrt(jnp.mean(o * o, axis=-1, keepdims=True) + SUBLN_EPS) * g_ref[...]
        o_ref[:, g * dh:(g + 1) * dh] = (y * (1.0 - lam_init)).astype(o_ref.dtype)


def _diff_attention(qkv, lam_params, subln_g, lam_init, batch, seq):
    m, three_d = qkv.shape
    d = three_d // 3
    dh = 2 * HEAD_DIM
    nh = d // dh
    hp = _heads_per_step(nh, 2)
    ng = nh // hp
    w = hp * dh
    t = _pick(seq, (512, 256, 128))
    nq = seq // t
    vmem = (2 * (2 * seq * w * 2) + 4 * t * w * 2 + 2 * hp * t * (2 * LANES + dh) * 4
            + 16 * ROW_BLOCK * t * 4 + 8 * MIB)
    kern = functools.partial(_diff_attn_kernel, t=t, hp=hp, lam_init=lam_init)
    return pl.pallas_call(
        kern,
        out_shape=jax.ShapeDtypeStruct((m, d), BF16),
        grid=(batch, ng, nq),
        in_specs=[pl.BlockSpec((4, HEAD_DIM), lambda b, h, i: (0, 0)),
                  pl.BlockSpec((1, dh), lambda b, h, i: (0, 0)),
                  pl.BlockSpec((t, w), lambda b, h, i: (b * nq + i, h)),
                  pl.BlockSpec((seq, w), lambda b, h, i: (b, ng + h)),
                  pl.BlockSpec((seq, w), lambda b, h, i: (b, 2 * ng + h))],
        out_specs=pl.BlockSpec((t, w), lambda b, h, i: (b * nq + i, h)),
        scratch_shapes=[pltpu.VMEM((2 * hp, t, LANES), F32), pltpu.VMEM((2 * hp, t, LANES), F32),
                        pltpu.VMEM((2 * hp, t, dh), F32)],
        compiler_params=_cparams(3, vmem),
        name="diff_attention",
    )(lam_params, subln_g.reshape(1, dh), qkv, qkv, qkv)


def _moba_attn_kernel(q_ref, k_ref, v_ref, o_ref, kmean_ref, selb_ref, m_ref, l_ref, acc_ref,
                      *, t, hp, blk, nbp, topk):
    i = pl.program_id(2)
    seq = k_ref.shape[0]
    bpc = t // blk
    blk_shift = blk.bit_length() - 1
    heads = [slice(g * HEAD_DIM, (g + 1) * HEAD_DIM) for g in range(hp)]

    @pl.when(i == 0)
    def _():
        r = lax.broadcasted_iota(I32, (nbp, seq), 0)
        c = lax.broadcasted_iota(I32, (nbp, seq), 1)
        avg = jnp.where(c >= r * blk, jnp.where(c < (r + 1) * blk, 1.0 / blk, 0.0), 0.0).astype(BF16)
        for g, sl in enumerate(heads):
            kmean_ref[g] = jnp.dot(avg, k_ref[:, sl], preferred_element_type=F32).astype(BF16)

    n_idx = lax.broadcasted_iota(I32, (nbp, t), 0)
    n_f = n_idx.astype(F32)
    qblk = lax.shift_right_logical(i * t + lax.broadcasted_iota(I32, (nbp, t), 1), blk_shift)
    valid = n_idx < qblk
    pad = jnp.full((LANES - nbp, t), NEG_INF, F32)
    for g, sl in enumerate(heads):
        gate = _qk(kmean_ref[g], q_ref[:, sl])
        gt = jnp.where(valid, gate, NEG_INF)
        picked = jnp.zeros((nbp, t), F32)
        for _ in range(topk):
            mx = jnp.max(gt, axis=0, keepdims=True)
            first = jnp.min(jnp.where(gt == mx, n_f, float(nbp)), axis=0, keepdims=True)
            hit = n_f == first
            picked = jnp.where(hit, 1.0, picked)
            gt = jnp.where(hit, NEG_INF, gt)
        sel_t = jnp.where(valid, jnp.where(picked > 0.5, 0.0, NEG_INF), NEG_INF)
        selb_ref[g] = jnp.concatenate([sel_t, pad], axis=0).T

    _flash_init(m_ref, l_ref, acc_ref)
    ones = jnp.ones((t, LANES), BF16)

    def block_bias(g, rows, r, j, width):
        sb = selb_ref[g, rows, :]
        nr = lax.broadcasted_iota(I32, (r, LANES), 1)
        col_blk = lax.shift_right_logical(lax.broadcasted_iota(I32, (r, width), 1), blk_shift)
        out = None
        for b in range(-(-width // blk)):
            col = jnp.max(jnp.where(nr == j * bpc + b, sb, NEG_INF), axis=-1, keepdims=True)
            out = col if out is None else jnp.where(col_blk >= b, col, out)
        return out

    diag_off = pl.multiple_of(i * t, t)
    for rb, rows, r in _row_blocks(t):
        width = (rb + 1) * r
        row = rb * r + lax.broadcasted_iota(I32, (r, width), 0)
        col = lax.broadcasted_iota(I32, (r, width), 1)
        own = lax.shift_right_logical(row, blk_shift) == lax.shift_right_logical(col, blk_shift)
        for g, sl in enumerate(heads):
            bias = jnp.where(col <= row, jnp.where(own, 0.0, block_bias(g, rows, r, i, width)), NEG_INF)
            v1 = jnp.concatenate([v_ref[pl.ds(diag_off, width), sl], ones[:width]], axis=1)
            _attend(q_ref[rows, sl], k_ref[pl.ds(diag_off, width), sl], v1, bias,
                    m_ref.at[g, rows], l_ref.at[g, rows], acc_ref.at[g, rows])

    def body(j, carry):
        off = pl.multiple_of(j * t, t)
        ks = [k_ref[pl.ds(off, t), sl] for sl in heads]
        v1s = [jnp.concatenate([v_ref[pl.ds(off, t), sl], ones], axis=1) for sl in heads]
        for rb, rows, r in _row_blocks(t):
            for g, sl in enumerate(heads):
                _attend(q_ref[rows, sl], ks[g], v1s[g], block_bias(g, rows, r, j, t),
                        m_ref.at[g, rows], l_ref.at[g, rows], acc_ref.at[g, rows])
        return carry

    lax.fori_loop(0, i, body, 0)
    for g, sl in enumerate(heads):
        o_ref[:, sl] = (acc_ref[g] * (1.0 / l_ref[g])).astype(o_ref.dtype)


def _moba_attention(qkv, batch, seq):
    m, three_d = qkv.shape
    d = three_d // 3
    nh = d // HEAD_DIM
    assert seq % MOBA_BLOCK == 0 and MOBA_BLOCK & (MOBA_BLOCK - 1) == 0
    t = _pick(seq, (512, 256))
    assert t % MOBA_BLOCK == 0 and MOBA_BLOCK % ROW_BLOCK == 0
    hp = _heads_per_step(nh, 4)
    ng = nh // hp
    w = hp * HEAD_DIM
    nq = seq // t
    nb = seq // MOBA_BLOCK
    nbp = -(-nb // 16) * 16
    assert nbp <= LANES
    vmem = (2 * (2 * seq * w * 2) + 4 * t * w * 2 + hp * t * 4 * LANES * 4
            + 16 * ROW_BLOCK * t * 4 + 8 * MIB)
    kern = functools.partial(_moba_attn_kernel, t=t, hp=hp, blk=MOBA_BLOCK, nbp=nbp,
                             topk=min(MOBA_TOPK, nb))
    return pl.pallas_call(
        kern,
        out_shape=jax.ShapeDtypeStruct((m, d), BF16),
        grid=(batch, ng, nq),
        in_specs=[pl.BlockSpec((t, w), lambda b, h, i: (b * nq + i, h)),
                  pl.BlockSpec((seq, w), lambda b, h, i: (b, ng + h)),
                  pl.BlockSpec((seq, w), lambda b, h, i: (b, 2 * ng + h))],
        out_specs=pl.BlockSpec((t, w), lambda b, h, i: (b * nq + i, h)),
        scratch_shapes=[pltpu.VMEM((hp, nbp, HEAD_DIM), BF16), pltpu.VMEM((hp, t, LANES), F32),
                        pltpu.VMEM((hp, t, LANES), F32), pltpu.VMEM((hp, t, LANES), F32),
                        pltpu.VMEM((hp, t, HEAD_DIM), F32)],
        compiler_params=_cparams(3, vmem),
        name="moba_attention",
    )(qkv, qkv, qkv)


def _dsa_index_kernel(qi_ref, ki_ref, w_ref, bias_ref, key_ref, wb_ref, *, tq, tk, nh, topk):
    i = pl.program_id(1)
    w = w_ref[...]
    for h in range(nh):
        wb_ref[h] = jnp.broadcast_to(w[:, h:h + 1], (tq, LANES))
    key_ref[...] = jnp.full(key_ref.shape, INT_MIN, I32)

    q_all = qi_ref[...].reshape(nh * tq, HEAD_DIM)
    qpos = i * tq + lax.broadcasted_iota(I32, (tq, tk), 0)
    col = lax.broadcasted_iota(I32, (tq, tk), 1)
    n_chunks = (i * tq + tq + tk - 1) // tk

    def score_chunk(c, carry):
        off = pl.multiple_of(c * tk, tk)
        kc = ki_ref[pl.ds(off, tk), :]
        r = _qk(q_all, kc)
        acc = jnp.zeros((tq, tk), F32)
        for h in range(nh):
            wbh = wb_ref[h]
            wfull = jnp.concatenate([wbh] * (tk // LANES), axis=1)
            acc = acc + jnp.maximum(r[h * tq:(h + 1) * tq], 0.0) * wfull
        bits = lax.bitcast_convert_type(acc, I32)
        key = bits ^ (lax.shift_right_arithmetic(bits, 31) & 0x7FFFFFFF)
        key_ref[c] = jnp.where(off + col <= qpos, key, INT_MIN)
        return carry

    lax.fori_loop(0, n_chunks, score_chunk, 0)

    def count_ge(cand):
        ge = jnp.where(key_ref[...] >= cand, 1.0, 0.0)
        return jnp.sum(jnp.sum(ge, axis=0), axis=-1, keepdims=True)

    zero = jnp.zeros((tq, 1), I32)
    thr = jnp.where(count_ge(zero) >= topk, zero, INT_MIN)

    def bisect(b, thr):
        cand = thr + lax.shift_left(jnp.int32(1), 30 - b)
        return jnp.where(count_ge(cand) >= topk, cand, thr)

    thr = lax.fori_loop(0, 31, bisect, thr)
    thr = jnp.maximum(thr, INT_MIN + 1)
    bias_ref[...] = jnp.where(key_ref[...] >= thr, 0.0, NEG_INF).astype(bias_ref.dtype)


def _dsa_index_bias(qi, ki, wts, batch, seq, tk):
    nh, m, _ = qi.shape
    tq = _pick(seq, (128,))
    nq = seq // tq
    nkc = seq // tk
    topk = min(DSA_TOPK_MAX, seq // 4)
    vmem = (2 * nh * tq * HEAD_DIM * 2 + 2 * seq * HEAD_DIM * 2 + 2 * tq * LANES * 4
            + 2 * tq * seq * 2 + tq * seq * 4 + nh * tq * LANES * 4
            + 2 * nh * tq * tk * 4 + 4 * tq * seq * 4 + 6 * MIB)
    kern = functools.partial(_dsa_index_kernel, tq=tq, tk=tk, nh=nh, topk=topk)
    return pl.pallas_call(
        kern,
        out_shape=jax.ShapeDtypeStruct((batch, nkc, seq, tk), BF16),
        grid=(batch, nq),
        in_specs=[pl.BlockSpec((nh, tq, HEAD_DIM), lambda b, i: (0, b * nq + i, 0)),
                  pl.BlockSpec((seq, HEAD_DIM), lambda b, i: (b, 0)),
                  pl.BlockSpec((tq, LANES), lambda b, i: (b * nq + i, 0))],
        out_specs=pl.BlockSpec((None, nkc, tq, tk), lambda b, i: (b, 0, i, 0)),
        scratch_shapes=[pltpu.VMEM((nkc, tq, tk), I32), pltpu.VMEM((nh, tq, LANES), F32)],
        compiler_params=_cparams(2, vmem),
        name="dsa_index_bias",
    )(qi, ki, wts)


def _dsa_attn_kernel(q_ref, k_ref, v_ref, bias_ref, o_ref, m_ref, l_ref, acc_ref, *, t, hp):
    i = pl.program_id(1)
    _flash_init(m_ref, l_ref, acc_ref)
    ones = jnp.ones((t, LANES), BF16)
    heads = [slice(g * HEAD_DIM, (g + 1) * HEAD_DIM) for g in range(hp)]

    diag_off = pl.multiple_of(i * t, t)
    for rb, rows, r in _row_blocks(t):
        width = (rb + 1) * r
        bias = bias_ref[i, rows, :width].astype(F32)
        for g, sl in enumerate(heads):
            v1 = jnp.concatenate([v_ref[pl.ds(diag_off, width), sl], ones[:width]], axis=1)
            _attend(q_ref[rows, sl], k_ref[pl.ds(diag_off, width), sl], v1, bias,
                    m_ref.at[g, rows], l_ref.at[g, rows], acc_ref.at[g, rows])

    def body(j, carry):
        off = pl.multiple_of(j * t, t)
        ks = [k_ref[pl.ds(off, t), sl] for sl in heads]
        v1s = [jnp.concatenate([v_ref[pl.ds(off, t), sl], ones], axis=1) for sl in heads]
        for rb, rows, r in _row_blocks(t):
            bias = bias_ref[j, rows, :].astype(F32)
            for g, sl in enumerate(heads):
                _attend(q_ref[rows, sl], ks[g], v1s[g], bias,
                        m_ref.at[g, rows], l_ref.at[g, rows], acc_ref.at[g, rows])
        return carry

    lax.fori_loop(0, i, body, 0)
    for g, sl in enumerate(heads):
        o_ref[:, sl] = (acc_ref[g] * (1.0 / l_ref[g])).astype(o_ref.dtype)


def _dsa_attention(qkv, bias, batch, seq, t):
    m, three_d = qkv.shape
    d = three_d // 3
    nh = d // HEAD_DIM
    hp = _heads_per_step(nh, 4)
    ng = nh // hp
    w = hp * HEAD_DIM
    nq = seq // t
    nkc = seq // t
    vmem = (2 * (2 * seq * w * 2) + 4 * t * w * 2 + 2 * nkc * t * t * 2
            + hp * t * 3 * LANES * 4 + 16 * ROW_BLOCK * t * 4 + 8 * MIB)
    kern = functools.partial(_dsa_attn_kernel, t=t, hp=hp)
    return pl.pallas_call(
        kern,
        out_shape=jax.ShapeDtypeStruct((m, d), BF16),
        grid=(batch, nq, ng),
        in_specs=[pl.BlockSpec((t, w), lambda b, i, h: (b * nq + i, h)),
                  pl.BlockSpec((seq, w), lambda b, i, h: (b, ng + h)),
                  pl.BlockSpec((seq, w), lambda b, i, h: (b, 2 * ng + h)),
                  pl.BlockSpec((None, nkc, t, t), lambda b, i, h: (b, 0, i, 0))],
        out_specs=pl.BlockSpec((t, w), lambda b, i, h: (b * nq + i, h)),
        scratch_shapes=[pltpu.VMEM((hp, t, LANES), F32), pltpu.VMEM((hp, t, LANES), F32),
                        pltpu.VMEM((hp, t, HEAD_DIM), F32)],
        compiler_params=_cparams(3, vmem),
        name="dsa_attention",
    )(qkv, qkv, qkv, bias)


def _lambda_init_for(layer):
    return 0.8 - 0.6 * math.exp(-0.3 * layer)


def _ffn(h, g, w_gate, w_up, w_down, widx):
    act = _swiglu_up(_rmsnorm(h, g, BF16), w_gate, w_up, widx)
    return _matmul_residual(act, w_down, widx, h, 0.5, 512)


def kernel(x, positions, norm_g, ffn_w_gate, ffn_w_up, ffn_w_down, attn_w_in, attn_w_out,
           diff_lambda_q1, diff_lambda_k1, diff_lambda_q2, diff_lambda_k2, diff_subln_g,
           idx_w_q, idx_w_k, idx_w_head, final_norm_g):
    batch, seq, d = x.shape
    depth = norm_g.shape[0]
    m = batch * seq
    tables = _rope_tables(positions)
    h = x.reshape(m, d)
    for i in range(depth):
        h = _ffn(h, norm_g[i, 0], ffn_w_gate, ffn_w_up, ffn_w_down, (i, 0))
        hn = _rmsnorm(h, norm_g[i, 1], BF16)
        qkv = _matmul_rope(hn, attn_w_in, (i,), tables, d, 2 * d, Q_PRESCALE)
        mixer = i % N_MIXERS
        j = i // N_MIXERS
        if mixer == 0:
            lam_params = jnp.stack([diff_lambda_q1[j], diff_lambda_k1[j],
                                    diff_lambda_q2[j], diff_lambda_k2[j]])
            mix = _diff_attention(qkv, lam_params, diff_subln_g[j], _lambda_init_for(i), batch, seq)
        elif mixer == 1:
            mix = _moba_attention(qkv, batch, seq)
        else:
            n_idx_heads = idx_w_head.shape[-1]
            idx_dim = idx_w_k.shape[-1]
            qi = _index_q_proj(hn, idx_w_q, (j,), tables)
            ki, wts = _index_kw_proj(hn, idx_w_k[j], idx_w_head[j], tables,
                                     n_idx_heads ** -0.5 * idx_dim ** -0.5)
            t = _pick(seq, (512, 256))
            bias = _dsa_index_bias(qi, ki, wts, batch, seq, t)
            mix = _dsa_attention(qkv, bias, batch, seq, t)
        h = _matmul_residual(mix, attn_w_out, (i,), h, 1.0, 1024)
        h = _ffn(h, norm_g[i, 2], ffn_w_gate, ffn_w_up, ffn_w_down, (i, 1))
    return _rmsnorm(h, final_norm_g, x.dtype).reshape(batch, seq, d)
```

```python
import functools
import math

import jax
import jax.numpy as jnp
from jax import lax
from jax.experimental import pallas as pl
from jax.experimental.pallas import tpu as pltpu

F32 = jnp.float32
BF16 = jnp.bfloat16
I32 = jnp.int32

HEAD_DIM = 128
ROT_DIM = HEAD_DIM // 4
ROT_HALF = ROT_DIM // 2
ROPE_THETA = 500000.0
RMS_EPS = 1e-6
SUBLN_EPS = 1e-5
MOBA_BLOCK = 256
MOBA_TOPK = 3
DSA_TOPK_MAX = 256
N_MIXERS = 3

LANES = 128
MIB = 1024 * 1024
VMEM_CAP_BYTES = 60 * MIB
NEG_INF = float("-inf")
INT_MIN = -(2 ** 31)


def _cparams(n_axes, vmem_bytes):
    return pltpu.CompilerParams(
        dimension_semantics=("arbitrary",) * n_axes,
        vmem_limit_bytes=int(min(VMEM_CAP_BYTES, vmem_bytes)))


def _pick(n, candidates):
    for c in candidates:
        if n % c == 0:
            return c
    raise ValueError(f"no tile in {candidates} divides {n}")


def _rmsnorm_kernel(x_ref, g_ref, o_ref, *, eps):
    x = x_ref[...]
    y = x * lax.rsqrt(jnp.mean(x * x, axis=-1, keepdims=True) + eps) * g_ref[...]
    o_ref[...] = y.astype(o_ref.dtype)


def _rmsnorm(x, g, out_dtype):
    m, d = x.shape
    tm = _pick(m, (256, 128, 8))
    vmem = 2 * tm * d * (4 + jnp.dtype(out_dtype).itemsize) + 4 * tm * d * 4 + 4 * MIB
    return pl.pallas_call(
        functools.partial(_rmsnorm_kernel, eps=RMS_EPS),
        out_shape=jax.ShapeDtypeStruct((m, d), out_dtype),
        grid=(m // tm,),
        in_specs=[pl.BlockSpec((tm, d), lambda i: (i, 0)),
                  pl.BlockSpec((1, d), lambda i: (0, 0))],
        out_specs=pl.BlockSpec((tm, d), lambda i: (i, 0)),
        compiler_params=_cparams(1, vmem),
        name="rmsnorm",
    )(x, g.reshape(1, d))


def _rope_table_kernel(pos_ref, inv_ref, c_ref, s1_ref, s2_ref):
    ang = pos_ref[...].astype(F32) * inv_ref[...]
    lane = lax.broadcasted_iota(I32, ang.shape, 1)
    cos = jnp.cos(ang)
    sin = jnp.sin(ang)
    c_ref[...] = jnp.where(lane < ROT_DIM, cos, 1.0)
    s1_ref[...] = jnp.where(lane < ROT_HALF, 0.0, jnp.where(lane < ROT_DIM, sin, 0.0))
    s2_ref[...] = jnp.where(lane < ROT_HALF, -sin, 0.0)


def _rope_tables(positions):
    m = positions.size
    inv = ROPE_THETA ** (-jnp.arange(0, ROT_DIM, 2, dtype=F32) / ROT_DIM)
    inv = jnp.concatenate([inv, inv, jnp.zeros((LANES - ROT_DIM,), F32)]).reshape(1, LANES)
    tm = _pick(m, (512, 256, 128, 8))
    spec = pl.BlockSpec((tm, LANES), lambda i: (i, 0))
    return pl.pallas_call(
        _rope_table_kernel,
        out_shape=[jax.ShapeDtypeStruct((m, LANES), F32)] * 3,
        grid=(m // tm,),
        in_specs=[pl.BlockSpec((tm, 1), lambda i: (i, 0)),
                  pl.BlockSpec((1, LANES), lambda i: (0, 0))],
        out_specs=[spec, spec, spec],
        compiler_params=_cparams(1, 16 * MIB),
        name="rope_tables",
    )(positions.reshape(m, 1).astype(I32), inv)


def _rope(a, c, s1, s2):
    return (a * c + pltpu.roll(a, ROT_HALF, 1) * s1
            + pltpu.roll(a, LANES - ROT_HALF, 1) * s2)


def _mm_vmem(tm, tn, k, mt, out_bytes, n_w=1, extra=0):
    return (2 * tm * k * 2 + n_w * (2 * k * tn * 2 + (k // mt) * tn * 4) + 2 * tm * tn * out_bytes
            + (n_w + 1) * tm * tn * 4 + extra + 8 * MIB)


class _WeightStream:
    def __init__(self, w_hbm, widx, stage_ref, wb_ref, sem):
        self.w_hbm, self.widx, self.stage, self.wb, self.sem = w_hbm, tuple(widx), stage_ref, wb_ref, sem
        self.kc, self.tn = stage_ref.shape
        self.n = pl.program_id(0)
        self.m = pl.program_id(1)
        self.have_next = self.n + 1 < pl.num_programs(0)

    def _copy(self, tile, chunk):
        rows = pl.ds(pl.multiple_of(chunk * self.kc, self.kc), self.kc)
        cols = pl.ds(pl.multiple_of(tile * self.tn, self.tn), self.tn)
        return pltpu.make_async_copy(self.w_hbm.at[self.widx + (rows, cols)], self.stage, self.sem)

    def _cast(self, tile, chunk):
        rows = pl.ds(pl.multiple_of(chunk * self.kc, self.kc), self.kc)
        self.wb[tile % 2, rows, :] = self.stage[...].astype(BF16)

    def begin(self):
        @pl.when((self.n == 0) & (self.m == 0))
        def _():
            def body(c, carry):
                cp = self._copy(0, c)
                cp.start()
                cp.wait()
                self._cast(0, c)
                return carry
            lax.fori_loop(0, pl.num_programs(1), body, 0)

        @pl.when(self.have_next)
        def _():
            self._copy(self.n + 1, self.m).start()

    def tile(self):
        return self.wb[self.n % 2]

    def finish(self):
        @pl.when(self.have_next)
        def _():
            self._copy(self.n + 1, self.m).wait()
            self._cast(self.n + 1, self.m)


def _stream_scratch(k, tn, mt):
    assert k % mt == 0 and (k // mt) % 16 == 0, (k, mt)
    return [pltpu.VMEM((k // mt, tn), F32), pltpu.VMEM((2, k, tn), BF16), pltpu.SemaphoreType.DMA(())]


_ANY_SPEC = pl.BlockSpec(memory_space=pl.ANY)


def _mm_rope_kernel(x_ref, w_hbm, c_ref, s1_ref, s2_ref, o_ref, stage_ref, wb_ref, sem,
                    *, widx, tn, n_q_tiles, n_rope_tiles, q_scale):
    j = pl.program_id(0)
    ws = _WeightStream(w_hbm, widx, stage_ref, wb_ref, sem)
    ws.begin()

    def matmul():
        return jnp.dot(x_ref[...], ws.tile(), preferred_element_type=F32)

    def roped(scale):
        acc = matmul()
        c, s1, s2 = c_ref[...], s1_ref[...], s2_ref[...]
        for g in range(tn // LANES):
            sl = slice(g * LANES, (g + 1) * LANES)
            y = _rope(acc[:, sl], c, s1, s2)
            o_ref[:, sl] = (y if scale is None else y * scale).astype(o_ref.dtype)

    @pl.when(j < n_q_tiles)
    def _():
        roped(q_scale)

    @pl.when((j >= n_q_tiles) & (j < n_rope_tiles))
    def _():
        roped(None)

    @pl.when(j >= n_rope_tiles)
    def _():
        o_ref[...] = matmul().astype(o_ref.dtype)

    ws.finish()


def _matmul_rope(x, w, widx, tables, n_q_cols, n_rope_cols, q_scale, out_dtype=BF16):
    m, k = x.shape
    n = w.shape[-1]
    tm = _pick(m, (1024, 512, 256, 128))
    tn = _pick(n, (1024, 512, 256, 128))
    mt = m // tm
    assert n_rope_cols % tn == 0 and n_q_cols % tn == 0 and n_q_cols <= n_rope_cols
    tab_spec = pl.BlockSpec((tm, LANES), lambda j, i: (i, 0))
    kern = functools.partial(_mm_rope_kernel, widx=widx, tn=tn, n_q_tiles=n_q_cols // tn,
                             n_rope_tiles=n_rope_cols // tn, q_scale=q_scale)
    return pl.pallas_call(
        kern,
        out_shape=jax.ShapeDtypeStruct((m, n), out_dtype),
        grid=(n // tn, mt),
        in_specs=[pl.BlockSpec((tm, k), lambda j, i: (i, 0)), _ANY_SPEC,
                  tab_spec, tab_spec, tab_spec],
        out_specs=pl.BlockSpec((tm, tn), lambda j, i: (i, j)),
        scratch_shapes=_stream_scratch(k, tn, mt),
        compiler_params=_cparams(2, _mm_vmem(tm, tn, k, mt, 2, extra=6 * tm * LANES * 4 + 2 * tm * tn * 4)),
        name="matmul_rope",
    )(x, w, *tables)


def _index_kw_proj(x, w_k, w_head, tables, head_scale):
    m, k = x.shape
    nh = w_head.shape[1]
    assert w_k.shape[1] == LANES and nh <= LANES
    w = jnp.concatenate([w_k, w_head, jnp.zeros((k, LANES - nh), w_k.dtype)], axis=1)
    tm = _pick(m, (1024, 512, 256, 128))
    tab_spec = pl.BlockSpec((tm, LANES), lambda i: (i, 0))
    out_spec = pl.BlockSpec((tm, LANES), lambda i: (i, 0))

    def kern(x_ref, w_ref, c_ref, s1_ref, s2_ref, ki_ref, wt_ref, wb_ref):
        @pl.when(pl.program_id(0) == 0)
        def _():
            wb_ref[...] = w_ref[...].astype(BF16)
        acc = jnp.dot(x_ref[...], wb_ref[...], preferred_element_type=F32)
        ki_ref[...] = _rope(acc[:, :LANES], c_ref[...], s1_ref[...], s2_ref[...]).astype(BF16)
        wt_ref[...] = acc[:, LANES:] * head_scale

    return pl.pallas_call(
        kern,
        out_shape=[jax.ShapeDtypeStruct((m, LANES), BF16), jax.ShapeDtypeStruct((m, LANES), F32)],
        grid=(m // tm,),
        in_specs=[pl.BlockSpec((tm, k), lambda i: (i, 0)),
                  pl.BlockSpec((k, 2 * LANES), lambda i: (0, 0)),
                  tab_spec, tab_spec, tab_spec],
        out_specs=[out_spec, out_spec],
        scratch_shapes=[pltpu.VMEM((k, 2 * LANES), BF16)],
        compiler_params=_cparams(1, 2 * tm * k * 2 + 2 * k * 2 * LANES * 4 + k * 2 * LANES * 2
                                 + 16 * tm * LANES * 4 + 4 * MIB),
        name="index_kw_proj",
    )(x, w, *tables)


def _index_q_proj(x, w, widx, tables):
    m, k = x.shape
    n = w.shape[-1]
    nh = n // LANES
    tm = _pick(m, (1024, 512, 256, 128))
    tn = _pick(n, (1024, 512, 256, 128))
    mt = m // tm
    hpt = tn // LANES
    tab_spec = pl.BlockSpec((tm, LANES), lambda j, i: (i, 0))

    def kern(x_ref, w_hbm, c_ref, s1_ref, s2_ref, o_ref, stage_ref, wb_ref, sem):
        ws = _WeightStream(w_hbm, widx, stage_ref, wb_ref, sem)
        ws.begin()
        acc = jnp.dot(x_ref[...], ws.tile(), preferred_element_type=F32)
        c, s1, s2 = c_ref[...], s1_ref[...], s2_ref[...]
        for g in range(hpt):
            o_ref[g] = _rope(acc[:, g * LANES:(g + 1) * LANES], c, s1, s2).astype(BF16)
        ws.finish()

    return pl.pallas_call(
        kern,
        out_shape=jax.ShapeDtypeStruct((nh, m, LANES), BF16),
        grid=(n // tn, mt),
        in_specs=[pl.BlockSpec((tm, k), lambda j, i: (i, 0)), _ANY_SPEC,
                  tab_spec, tab_spec, tab_spec],
        out_specs=pl.BlockSpec((hpt, tm, LANES), lambda j, i: (j, i, 0)),
        scratch_shapes=_stream_scratch(k, tn, mt),
        compiler_params=_cparams(2, _mm_vmem(tm, tn, k, mt, 2, extra=6 * tm * LANES * 4 + 2 * tm * tn * 4)),
        name="index_q_proj",
    )(x, w, *tables)


def _swiglu_kernel(x_ref, wg_hbm, wu_hbm, o_ref, gstage_ref, gb_ref, gsem, ustage_ref, ub_ref, usem, *, widx):
    wgs = _WeightStream(wg_hbm, widx, gstage_ref, gb_ref, gsem)
    wus = _WeightStream(wu_hbm, widx, ustage_ref, ub_ref, usem)
    wgs.begin()
    wus.begin()
    x = x_ref[...]
    g = jnp.dot(x, wgs.tile(), preferred_element_type=F32)
    u = jnp.dot(x, wus.tile(), preferred_element_type=F32)
    o_ref[...] = (g * jax.nn.sigmoid(g) * u).astype(o_ref.dtype)
    wgs.finish()
    wus.finish()


def _swiglu_up(x, wg, wu, widx):
    m, k = x.shape
    n = wg.shape[-1]
    tm = _pick(m, (1024, 512, 256, 128))
    tn = _pick(n, (512, 256, 128))
    mt = m // tm
    return pl.pallas_call(
        functools.partial(_swiglu_kernel, widx=widx),
        out_shape=jax.ShapeDtypeStruct((m, n), BF16),
        grid=(n // tn, mt),
        in_specs=[pl.BlockSpec((tm, k), lambda j, i: (i, 0)), _ANY_SPEC, _ANY_SPEC],
        out_specs=pl.BlockSpec((tm, tn), lambda j, i: (i, j)),
        scratch_shapes=_stream_scratch(k, tn, mt) + _stream_scratch(k, tn, mt),
        compiler_params=_cparams(2, _mm_vmem(tm, tn, k, mt, 2, n_w=2)),
        name="swiglu_up",
    )(x, wg, wu)


def _mm_residual_kernel(x_ref, w_hbm, r_ref, o_ref, stage_ref, wb_ref, sem, *, widx, scale):
    ws = _WeightStream(w_hbm, widx, stage_ref, wb_ref, sem)
    ws.begin()
    acc = jnp.dot(x_ref[...], ws.tile(), preferred_element_type=F32)
    if scale != 1.0:
        acc = acc * scale
    o_ref[...] = r_ref[...] + acc
    ws.finish()


def _matmul_residual(x, w, widx, res, scale):
    m, k = x.shape
    n = w.shape[-1]
    tn = _pick(n, (1024, 512, 256, 128))
    tm = _pick(m, (512, 256, 128))
    mt = m // tm
    return pl.pallas_call(
        functools.partial(_mm_residual_kernel, widx=widx, scale=scale),
        out_shape=jax.ShapeDtypeStruct((m, n), F32),
        grid=(n // tn, mt),
        in_specs=[pl.BlockSpec((tm, k), lambda j, i: (i, 0)), _ANY_SPEC,
                  pl.BlockSpec((tm, tn), lambda j, i: (i, j))],
        out_specs=pl.BlockSpec((tm, tn), lambda j, i: (i, j)),
        scratch_shapes=_stream_scratch(k, tn, mt),
        compiler_params=_cparams(2, _mm_vmem(tm, tn, k, mt, 4, extra=2 * tm * tn * 4)),
        name="matmul_residual",
    )(x, w, res)


ROW_BLOCK = 128
LOG2E = math.log2(math.e)
Q_PRESCALE = HEAD_DIM ** -0.5 * LOG2E


def _flash_init(m_ref, l_ref, acc_ref):
    m_ref[...] = jnp.full(m_ref.shape, NEG_INF, F32)
    l_ref[...] = jnp.zeros(l_ref.shape, F32)
    acc_ref[...] = jnp.zeros(acc_ref.shape, F32)


def _qk(q, k):
    return lax.dot_general(q, k, (((1,), (1,)), ((), ())), preferred_element_type=F32)


def _lane_tile(x, n):
    return x if n == 1 else jnp.concatenate([x] * n, axis=1)


def _softmax_update(s, m_ref):
    m_prev = m_ref[...]
    m_new = jnp.maximum(m_prev, jnp.max(s, axis=-1, keepdims=True))
    m_safe = jnp.where(m_new == NEG_INF, 0.0, m_new)
    alpha = jnp.exp2(m_prev - m_safe)
    p = jnp.exp2(s - _lane_tile(m_safe, s.shape[1] // LANES))
    m_ref[...] = m_new
    return p, alpha


def _attend(q, k, v1, bias, m_ref, l_ref, acc_ref):
    s = _qk(q, k)
    if bias is not None:
        s = s + bias
    p, alpha = _softmax_update(s, m_ref)
    pv = jnp.dot(p.astype(BF16), v1, preferred_element_type=F32)
    dv = acc_ref.shape[-1]
    l_ref[...] = alpha * l_ref[...] + pv[:, dv:]
    acc_ref[...] = alpha * acc_ref[...] + pv[:, :dv]


def _row_blocks(t):
    r = min(ROW_BLOCK, t)
    return [(rb, slice(rb * r, (rb + 1) * r), r) for rb in range(t // r)]


def _heads_per_step(n_heads, preferred):
    for c in (preferred, 2, 1):
        if c <= preferred and n_heads % c == 0:
            return c
    return 1


def _diff_attn_kernel(lam_ref, g_ref, q_ref, k_ref, v_ref, o_ref, m_ref, l_ref, acc_ref,
                      *, t, hp, lam_init):
    i = pl.program_id(2)
    dh = 2 * HEAD_DIM
    _flash_init(m_ref, l_ref, acc_ref)

    def step(off, width, rows, r, g, bias):
        v = v_ref[pl.ds(off, width), g * dh:(g + 1) * dh]
        ps, alphas = [], []
        for c in range(2):
            cols = slice(g * dh + c * HEAD_DIM, g * dh + (c + 1) * HEAD_DIM)
            s = _qk(q_ref[rows, cols], k_ref[pl.ds(off, width), cols])
            if bias is not None:
                s = s + bias
            slot = 2 * g + c
            p, alpha = _softmax_update(s, m_ref.at[slot, rows])
            l_ref[slot, rows] = alpha * l_ref[slot, rows] + jnp.sum(p, axis=-1, keepdims=True)
            ps.append(p.astype(BF16))
            alphas.append(alpha)
        pv = jnp.dot(jnp.concatenate(ps, axis=0), v, preferred_element_type=F32)
        for c in range(2):
            slot = 2 * g + c
            acc_ref[slot, rows] = (_lane_tile(alphas[c], dh // LANES) * acc_ref[slot, rows]
                                   + pv[c * r:(c + 1) * r])

    diag_off = pl.multiple_of(i * t, t)
    for rb, rows, r in _row_blocks(t):
        width = (rb + 1) * r
        row = rb * r + lax.broadcasted_iota(I32, (r, width), 0)
        col = lax.broadcasted_iota(I32, (r, width), 1)
        bias = jnp.where(col <= row, 0.0, NEG_INF)
        for g in range(hp):
            step(diag_off, width, rows, r, g, bias)

    def body(j, carry):
        off = pl.multiple_of(j * t, t)
        for rb, rows, r in _row_blocks(t):
            for g in range(hp):
                step(off, t, rows, r, g, None)
        return carry

    lax.fori_loop(0, i, body, 0)

    p = lam_ref[...]
    lam = (jnp.exp(jnp.sum(p[0:1] * p[1:2], axis=-1, keepdims=True))
           - jnp.exp(jnp.sum(p[2:3] * p[3:4], axis=-1, keepdims=True)) + lam_init)
    for g in range(hp):
        w1 = _lane_tile(1.0 / l_ref[2 * g], dh // LANES)
        w2 = _lane_tile(lam / l_ref[2 * g + 1], dh // LANES)
        o = acc_ref[2 * g] * w1 - acc_ref[2 * g + 1] * w2
        y = o * lax.rsqrt(jnp.mean(o * o, axis=-1, keepdims=True) + SUBLN_EPS) * g_ref[...]
        o_ref[:, g * dh:(g + 1) * dh] = (y * (1.0 - lam_init)).astype(o_ref.dtype)


def _diff_attention(qkv, lam_params, subln_g, lam_init, batch, seq):
    m, three_d = qkv.shape
    d = three_d // 3
    dh = 2 * HEAD_DIM
    nh = d // dh
    hp = _heads_per_step(nh, 2)
    ng = nh // hp
    w = hp * dh
    t = _pick(seq, (512, 256, 128))
    nq = seq // t
    vmem = (2 * (2 * seq * w * 2) + 4 * t * w * 2 + 2 * hp * t * (2 * LANES + dh) * 4
            + 16 * ROW_BLOCK * t * 4 + 8 * MIB)
    kern = functools.partial(_diff_attn_kernel, t=t, hp=hp, lam_init=lam_init)
    return pl.pallas_call(
        kern,
        out_shape=jax.ShapeDtypeStruct((m, d), BF16),
        grid=(batch, ng, nq),
        in_specs=[pl.BlockSpec((4, HEAD_DIM), lambda b, h, i: (0, 0)),
                  pl.BlockSpec((1, dh), lambda b, h, i: (0, 0)),
                  pl.BlockSpec((t, w), lambda b, h, i: (b * nq + i, h)),
                  pl.BlockSpec((seq, w), lambda b, h, i: (b, ng + h)),
                  pl.BlockSpec((seq, w), lambda b, h, i: (b, 2 * ng + h))],
        out_specs=pl.BlockSpec((t, w), lambda b, h, i: (b * nq + i, h)),
        scratch_shapes=[pltpu.VMEM((2 * hp, t, LANES), F32), pltpu.VMEM((2 * hp, t, LANES), F32),
                        pltpu.VMEM((2 * hp, t, dh), F32)],
        compiler_params=_cparams(3, vmem),
        name="diff_attention",
    )(lam_params, subln_g.reshape(1, dh), qkv, qkv, qkv)


def _moba_attn_kernel(q_ref, k_ref, v_ref, o_ref, kmean_ref, selb_ref, m_ref, l_ref, acc_ref,
                      *, t, hp, blk, nbp, topk):
    i = pl.program_id(2)
    seq = k_ref.shape[0]
    bpc = t // blk
    blk_shift = blk.bit_length() - 1
    heads = [slice(g * HEAD_DIM, (g + 1) * HEAD_DIM) for g in range(hp)]

    @pl.when(i == 0)
    def _():
        r = lax.broadcasted_iota(I32, (nbp, seq), 0)
        c = lax.broadcasted_iota(I32, (nbp, seq), 1)
        avg = jnp.where(c >= r * blk, jnp.where(c < (r + 1) * blk, 1.0 / blk, 0.0), 0.0).astype(BF16)
        for g, sl in enumerate(heads):
            kmean_ref[g] = jnp.dot(avg, k_ref[:, sl], preferred_element_type=F32).astype(BF16)

    n_idx = lax.broadcasted_iota(I32, (nbp, t), 0)
    n_f = n_idx.astype(F32)
    qblk = lax.shift_right_logical(i * t + lax.broadcasted_iota(I32, (nbp, t), 1), blk_shift)
    valid = n_idx < qblk
    pad = jnp.full((LANES - nbp, t), NEG_INF, F32)
    for g, sl in enumerate(heads):
        gate = _qk(kmean_ref[g], q_ref[:, sl])
        gt = jnp.where(valid, gate, NEG_INF)
        picked = jnp.zeros((nbp, t), F32)
        for _ in range(topk):
            mx = jnp.max(gt, axis=0, keepdims=True)
            first = jnp.min(jnp.where(gt == mx, n_f, float(nbp)), axis=0, keepdims=True)
            hit = n_f == first
            picked = jnp.where(hit, 1.0, picked)
            gt = jnp.where(hit, NEG_INF, gt)
        sel_t = jnp.where(valid, jnp.where(picked > 0.5, 0.0, NEG_INF), NEG_INF)
        selb_ref[g] = jnp.concatenate([sel_t, pad], axis=0).T

    _flash_init(m_ref, l_ref, acc_ref)
    ones = jnp.ones((t, LANES), BF16)

    def block_bias(g, rows, r, j, width):
        sb = selb_ref[g, rows, :]
        nr = lax.broadcasted_iota(I32, (r, LANES), 1)
        col_blk = lax.shift_right_logical(lax.broadcasted_iota(I32, (r, width), 1), blk_shift)
        out = None
        for b in range(-(-width // blk)):
            col = jnp.max(jnp.where(nr == j * bpc + b, sb, NEG_INF), axis=-1, keepdims=True)
            out = col if out is None else jnp.where(col_blk >= b, col, out)
        return out

    diag_off = pl.multiple_of(i * t, t)
    for rb, rows, r in _row_blocks(t):
        width = (rb + 1) * r
        row = rb * r + lax.broadcasted_iota(I32, (r, width), 0)
        col = lax.broadcasted_iota(I32, (r, width), 1)
        own = lax.shift_right_logical(row, blk_shift) == lax.shift_right_logical(col, blk_shift)
        for g, sl in enumerate(heads):
            bias = jnp.where(col <= row, jnp.where(own, 0.0, block_bias(g, rows, r, i, width)), NEG_INF)
            v1 = jnp.concatenate([v_ref[pl.ds(diag_off, width), sl], ones[:width]], axis=1)
            _attend(q_ref[rows, sl], k_ref[pl.ds(diag_off, width), sl], v1, bias,
                    m_ref.at[g, rows], l_ref.at[g, rows], acc_ref.at[g, rows])

    def body(j, carry):
        off = pl.multiple_of(j * t, t)
        ks = [k_ref[pl.ds(off, t), sl] for sl in heads]
        v1s = [jnp.concatenate([v_ref[pl.ds(off, t), sl], ones], axis=1) for sl in heads]
        for rb, rows, r in _row_blocks(t):
            for g, sl in enumerate(heads):
                _attend(q_ref[rows, sl], ks[g], v1s[g], block_bias(g, rows, r, j, t),
                        m_ref.at[g, rows], l_ref.at[g, rows], acc_ref.at[g, rows])
        return carry

    lax.fori_loop(0, i, body, 0)
    for g, sl in enumerate(heads):
        o_ref[:, sl] = (acc_ref[g] * (1.0 / l_ref[g])).astype(o_ref.dtype)


def _moba_attention(qkv, batch, seq):
    m, three_d = qkv.shape
    d = three_d // 3
    nh = d // HEAD_DIM
    assert seq % MOBA_BLOCK == 0 and MOBA_BLOCK & (MOBA_BLOCK - 1) == 0
    t = _pick(seq, (512, 256))
    assert t % MOBA_BLOCK == 0 and MOBA_BLOCK % ROW_BLOCK == 0
    hp = _heads_per_step(nh, 4)
    ng = nh // hp
    w = hp * HEAD_DIM
    nq = seq // t
    nb = seq // MOBA_BLOCK
    nbp = -(-nb // 16) * 16
    assert nbp <= LANES
    vmem = (2 * (2 * seq * w * 2) + 4 * t * w * 2 + hp * t * 4 * LANES * 4
            + 16 * ROW_BLOCK * t * 4 + 8 * MIB)
    kern = functools.partial(_moba_attn_kernel, t=t, hp=hp, blk=MOBA_BLOCK, nbp=nbp,
                             topk=min(MOBA_TOPK, nb))
    return pl.pallas_call(
        kern,
        out_shape=jax.ShapeDtypeStruct((m, d), BF16),
        grid=(batch, ng, nq),
        in_specs=[pl.BlockSpec((t, w), lambda b, h, i: (b * nq + i, h)),
                  pl.BlockSpec((seq, w), lambda b, h, i: (b, ng + h)),
                  pl.BlockSpec((seq, w), lambda b, h, i: (b, 2 * ng + h))],
        out_specs=pl.BlockSpec((t, w), lambda b, h, i: (b * nq + i, h)),
        scratch_shapes=[pltpu.VMEM((hp, nbp, HEAD_DIM), BF16), pltpu.VMEM((hp, t, LANES), F32),
                        pltpu.VMEM((hp, t, LANES), F32), pltpu.VMEM((hp, t, LANES), F32),
                        pltpu.VMEM((hp, t, HEAD_DIM), F32)],
        compiler_params=_cparams(3, vmem),
        name="moba_attention",
    )(qkv, qkv, qkv)


def _dsa_index_kernel(qi_ref, ki_ref, w_ref, bias_ref, key_ref, wb_ref, *, tq, tk, nh, topk):
    i = pl.program_id(1)
    w = w_ref[...]
    for h in range(nh):
        wb_ref[h] = jnp.broadcast_to(w[:, h:h + 1], (tq, LANES))

    q_all = qi_ref[...].reshape(nh * tq, HEAD_DIM)
    qpos = i * tq + lax.broadcasted_iota(I32, (tq, tk), 0)
    col = lax.broadcasted_iota(I32, (tq, tk), 1)
    n_chunks = (i * tq + tq + tk - 1) // tk

    def score_chunk(c, carry):
        off = pl.multiple_of(c * tk, tk)
        kc = ki_ref[pl.ds(off, tk), :]
        r = _qk(q_all, kc)
        acc = jnp.zeros((tq, tk), F32)
        for h in range(nh):
            wbh = wb_ref[h]
            wfull = jnp.concatenate([wbh] * (tk // LANES), axis=1)
            acc = acc + jnp.maximum(r[h * tq:(h + 1) * tq], 0.0) * wfull
        bits = lax.bitcast_convert_type(acc + 0.0, I32)
        key = bits ^ (lax.shift_right_arithmetic(bits, 31) & 0x7FFFFFFF)
        key_ref[c] = jnp.where(off + col <= qpos, key, INT_MIN)
        return carry

    lax.fori_loop(0, n_chunks, score_chunk, 0)

    def count_ge(cand):
        def add(c, part):
            ge = jnp.where(key_ref[c] >= cand, 1.0, 0.0)
            return part + sum(ge[:, g * LANES:(g + 1) * LANES] for g in range(tk // LANES))
        part = lax.fori_loop(0, n_chunks, add, jnp.zeros((tq, LANES), F32))
        return jnp.sum(part, axis=-1, keepdims=True)

    zero = jnp.zeros((tq, 1), I32)
    thr = jnp.where(count_ge(zero) >= topk, zero, INT_MIN)

    def bisect(b, thr):
        cand = thr + lax.shift_left(jnp.int32(1), 30 - b)
        return jnp.where(count_ge(cand) >= topk, cand, thr)

    thr = lax.fori_loop(0, 31, bisect, thr)
    thr = jnp.maximum(thr, INT_MIN + 1)
    has_ties = jnp.max(count_ge(thr)) > topk

    bias_ref[...] = jnp.full(bias_ref.shape, NEG_INF, bias_ref.dtype)

    @pl.when(jnp.logical_not(has_ties))
    def _():
        def emit(c, carry):
            bias_ref[c] = jnp.where(key_ref[c] >= thr, 0.0, NEG_INF).astype(bias_ref.dtype)
            return carry
        lax.fori_loop(0, n_chunks, emit, 0)

    @pl.when(has_ties)
    def _():
        need = topk - count_ge(thr + 1)
        upper = jnp.where(lax.broadcasted_iota(I32, (tk, tk), 0) <= lax.broadcasted_iota(I32, (tk, tk), 1),
                          1.0, 0.0).astype(BF16)

        def emit(c, seen):
            key = key_ref[c]
            tied = key == thr
            rank = seen + jnp.dot(jnp.where(tied, 1.0, 0.0).astype(BF16), upper,
                                  preferred_element_type=F32)
            keep = jnp.where(tied, jnp.where(rank <= need, 0.0, NEG_INF), NEG_INF)
            bias_ref[c] = jnp.where(key > thr, 0.0, keep).astype(bias_ref.dtype)
            return rank[:, tk - 1:tk]

        lax.fori_loop(0, n_chunks, emit, jnp.zeros((tq, 1), F32))


def _dsa_index_bias(qi, ki, wts, batch, seq, tk):
    nh, m, _ = qi.shape
    tq = _pick(seq, (128,))
    nq = seq // tq
    nkc = seq // tk
    topk = min(DSA_TOPK_MAX, seq // 4)
    vmem = (2 * nh * tq * HEAD_DIM * 2 + 2 * seq * HEAD_DIM * 2 + 2 * tq * LANES * 4
            + 2 * tq * seq * 2 + tq * seq * 4 + nh * tq * LANES * 4
            + 2 * nh * tq * tk * 4 + 4 * tq * seq * 4 + 6 * MIB)
    kern = functools.partial(_dsa_index_kernel, tq=tq, tk=tk, nh=nh, topk=topk)
    return pl.pallas_call(
        kern,
        out_shape=jax.ShapeDtypeStruct((batch, nkc, seq, tk), BF16),
        grid=(batch, nq),
        in_specs=[pl.BlockSpec((nh, tq, HEAD_DIM), lambda b, i: (0, b * nq + i, 0)),
                  pl.BlockSpec((seq, HEAD_DIM), lambda b, i: (b, 0)),
                  pl.BlockSpec((tq, LANES), lambda b, i: (b * nq + i, 0))],
        out_specs=pl.BlockSpec((None, nkc, tq, tk), lambda b, i: (b, 0, i, 0)),
        scratch_shapes=[pltpu.VMEM((nkc, tq, tk), I32), pltpu.VMEM((nh, tq, LANES), F32)],
        compiler_params=_cparams(2, vmem),
        name="dsa_index_bias",
    )(qi, ki, wts)


def _dsa_attn_kernel(q_ref, k_ref, v_ref, bias_ref, o_ref, m_ref, l_ref, acc_ref, *, t, hp):
    i = pl.program_id(1)
    _flash_init(m_ref, l_ref, acc_ref)
    ones = jnp.ones((t, LANES), BF16)
    heads = [slice(g * HEAD_DIM, (g + 1) * HEAD_DIM) for g in range(hp)]

    diag_off = pl.multiple_of(i * t, t)
    for rb, rows, r in _row_blocks(t):
        width = (rb + 1) * r
        bias = bias_ref[i, rows, :width].astype(F32)
        for g, sl in enumerate(heads):
            v1 = jnp.concatenate([v_ref[pl.ds(diag_off, width), sl], ones[:width]], axis=1)
            _attend(q_ref[rows, sl], k_ref[pl.ds(diag_off, width), sl], v1, bias,
                    m_ref.at[g, rows], l_ref.at[g, rows], acc_ref.at[g, rows])

    def body(j, carry):
        off = pl.multiple_of(j * t, t)
        ks = [k_ref[pl.ds(off, t), sl] for sl in heads]
        v1s = [jnp.concatenate([v_ref[pl.ds(off, t), sl], ones], axis=1) for sl in heads]
        for rb, rows, r in _row_blocks(t):
            bias = bias_ref[j, rows, :].astype(F32)
            for g, sl in enumerate(heads):
                _attend(q_ref[rows, sl], ks[g], v1s[g], bias,
                        m_ref.at[g, rows], l_ref.at[g, rows], acc_ref.at[g, rows])
        return carry

    lax.fori_loop(0, i, body, 0)
    for g, sl in enumerate(heads):
        o_ref[:, sl] = (acc_ref[g] * (1.0 / l_ref[g])).astype(o_ref.dtype)


def _dsa_attention(qkv, bias, batch, seq, t):
    m, three_d = qkv.shape
    d = three_d // 3
    nh = d // HEAD_DIM
    hp = _heads_per_step(nh, 4)
    ng = nh // hp
    w = hp * HEAD_DIM
    nq = seq // t
    nkc = seq // t
    vmem = (2 * (2 * seq * w * 2) + 4 * t * w * 2 + 2 * nkc * t * t * 2
            + hp * t * 3 * LANES * 4 + 16 * ROW_BLOCK * t * 4 + 8 * MIB)
    kern = functools.partial(_dsa_attn_kernel, t=t, hp=hp)
    return pl.pallas_call(
        kern,
        out_shape=jax.ShapeDtypeStruct((m, d), BF16),
        grid=(batch, nq, ng),
        in_specs=[pl.BlockSpec((t, w), lambda b, i, h: (b * nq + i, h)),
                  pl.BlockSpec((seq, w), lambda b, i, h: (b, ng + h)),
                  pl.BlockSpec((seq, w), lambda b, i, h: (b, 2 * ng + h)),
                  pl.BlockSpec((None, nkc, t, t), lambda b, i, h: (b, 0, i, 0))],
        out_specs=pl.BlockSpec((t, w), lambda b, i, h: (b * nq + i, h)),
        scratch_shapes=[pltpu.VMEM((hp, t, LANES), F32), pltpu.VMEM((hp, t, LANES), F32),
                        pltpu.VMEM((hp, t, HEAD_DIM), F32)],
        compiler_params=_cparams(3, vmem),
        name="dsa_attention",
    )(qkv, qkv, qkv, bias)


def _lambda_init_for(layer):
    return 0.8 - 0.6 * math.exp(-0.3 * layer)


def _ffn(h, g, w_gate, w_up, w_down, widx):
    act = _swiglu_up(_rmsnorm(h, g, BF16), w_gate, w_up, widx)
    return _matmul_residual(act, w_down, widx, h, 0.5)


def kernel(x, positions, norm_g, ffn_w_gate, ffn_w_up, ffn_w_down, attn_w_in, attn_w_out,
           diff_lambda_q1, diff_lambda_k1, diff_lambda_q2, diff_lambda_k2, diff_subln_g,
           idx_w_q, idx_w_k, idx_w_head, final_norm_g):
    batch, seq, d = x.shape
    depth = norm_g.shape[0]
    m = batch * seq
    tables = _rope_tables(positions)
    h = x.reshape(m, d)
    for i in range(depth):
        h = _ffn(h, norm_g[i, 0], ffn_w_gate, ffn_w_up, ffn_w_down, (i, 0))
        hn = _rmsnorm(h, norm_g[i, 1], BF16)
        qkv = _matmul_rope(hn, attn_w_in, (i,), tables, d, 2 * d, Q_PRESCALE)
        mixer = i % N_MIXERS
        j = i // N_MIXERS
        if mixer == 0:
            lam_params = jnp.stack([diff_lambda_q1[j], diff_lambda_k1[j],
                                    diff_lambda_q2[j], diff_lambda_k2[j]])
            mix = _diff_attention(qkv, lam_params, diff_subln_g[j], _lambda_init_for(i), batch, seq)
        elif mixer == 1:
            mix = _moba_attention(qkv, batch, seq)
        else:
            n_idx_heads = idx_w_head.shape[-1]
            idx_dim = idx_w_k.shape[-1]
            qi = _index_q_proj(hn, idx_w_q, (j,), tables)
            ki, wts = _index_kw_proj(hn, idx_w_k[j], idx_w_head[j], tables,
                                     n_idx_heads ** -0.5 * idx_dim ** -0.5)
            t = _pick(seq, (512, 256))
            bias = _dsa_index_bias(qi, ki, wts, batch, seq, t)
            mix = _dsa_attention(qkv, bias, batch, seq, t)
        h = _matmul_residual(mix, attn_w_out, (i,), h, 1.0)
        h = _ffn(h, norm_g[i, 2], ffn_w_gate, ffn_w_up, ffn_w_down, (i, 1))
    return _rmsnorm(h, final_norm_g, x.dtype).reshape(batch, seq, d)
```

```python
import functools
import math

import jax
import jax.numpy as jnp
from jax import lax
from jax.experimental import pallas as pl
from jax.experimental.pallas import tpu as pltpu

F32 = jnp.float32
BF16 = jnp.bfloat16
I32 = jnp.int32

HEAD_DIM = 128
ROT_DIM = HEAD_DIM // 4
ROT_HALF = ROT_DIM // 2
ROPE_THETA = 500000.0
RMS_EPS = 1e-6
SUBLN_EPS = 1e-5
MOBA_BLOCK = 256
MOBA_TOPK = 3
DSA_TOPK_MAX = 256
N_MIXERS = 3

LANES = 128
MIB = 1024 * 1024
VMEM_CAP_BYTES = 60 * MIB
NEG_INF = float("-inf")
INT_MIN = -(2 ** 31)


def _cparams(n_axes, vmem_bytes):
    return pltpu.CompilerParams(
        dimension_semantics=("arbitrary",) * n_axes,
        vmem_limit_bytes=int(min(VMEM_CAP_BYTES, vmem_bytes)))


def _pick(n, candidates):
    for c in candidates:
        if n % c == 0:
            return c
    raise ValueError(f"no tile in {candidates} divides {n}")


def _rmsnorm_kernel(x_ref, g_ref, o_ref, *, eps):
    x = x_ref[...]
    y = x * lax.rsqrt(jnp.mean(x * x, axis=-1, keepdims=True) + eps) * g_ref[...]
    o_ref[...] = y.astype(o_ref.dtype)


def _rmsnorm(x, g, out_dtype):
    m, d = x.shape
    tm = _pick(m, (256, 128, 8))
    vmem = 2 * tm * d * (4 + jnp.dtype(out_dtype).itemsize) + 4 * tm * d * 4 + 4 * MIB
    return pl.pallas_call(
        functools.partial(_rmsnorm_kernel, eps=RMS_EPS),
        out_shape=jax.ShapeDtypeStruct((m, d), out_dtype),
        grid=(m // tm,),
        in_specs=[pl.BlockSpec((tm, d), lambda i: (i, 0)),
                  pl.BlockSpec((1, d), lambda i: (0, 0))],
        out_specs=pl.BlockSpec((tm, d), lambda i: (i, 0)),
        compiler_params=_cparams(1, vmem),
        name="rmsnorm",
    )(x, g.reshape(1, d))


def _rope_table_kernel(pos_ref, inv_ref, c_ref, s1_ref, s2_ref):
    ang = pos_ref[...].astype(F32) * inv_ref[...]
    lane = lax.broadcasted_iota(I32, ang.shape, 1)
    cos = jnp.cos(ang)
    sin = jnp.sin(ang)
    c_ref[...] = jnp.where(lane < ROT_DIM, cos, 1.0)
    s1_ref[...] = jnp.where(lane < ROT_HALF, 0.0, jnp.where(lane < ROT_DIM, sin, 0.0))
    s2_ref[...] = jnp.where(lane < ROT_HALF, -sin, 0.0)


def _rope_tables(positions):
    m = positions.size
    inv = ROPE_THETA ** (-jnp.arange(0, ROT_DIM, 2, dtype=F32) / ROT_DIM)
    inv = jnp.concatenate([inv, inv, jnp.zeros((LANES - ROT_DIM,), F32)]).reshape(1, LANES)
    tm = _pick(m, (512, 256, 128, 8))
    spec = pl.BlockSpec((tm, LANES), lambda i: (i, 0))
    return pl.pallas_call(
        _rope_table_kernel,
        out_shape=[jax.ShapeDtypeStruct((m, LANES), F32)] * 3,
        grid=(m // tm,),
        in_specs=[pl.BlockSpec((tm, 1), lambda i: (i, 0)),
                  pl.BlockSpec((1, LANES), lambda i: (0, 0))],
        out_specs=[spec, spec, spec],
        compiler_params=_cparams(1, 16 * MIB),
        name="rope_tables",
    )(positions.reshape(m, 1).astype(I32), inv)


def _rope(a, c, s1, s2):
    return (a * c + pltpu.roll(a, ROT_HALF, 1) * s1
            + pltpu.roll(a, LANES - ROT_HALF, 1) * s2)


def _mm_vmem(tm, tn, k, mt, out_bytes, n_w=1, extra=0):
    return (2 * tm * k * 2 + n_w * (2 * k * tn * 2 + (k // mt) * tn * 4) + 2 * tm * tn * out_bytes
            + (n_w + 1) * tm * tn * 4 + extra + 8 * MIB)


class _WeightStream:
    def __init__(self, w_hbm, widx, stage_ref, wb_ref, sem):
        self.w_hbm, self.widx, self.stage, self.wb, self.sem = w_hbm, tuple(widx), stage_ref, wb_ref, sem
        self.kc, self.tn = stage_ref.shape
        self.n = pl.program_id(0)
        self.m = pl.program_id(1)
        self.have_next = self.n + 1 < pl.num_programs(0)

    def _copy(self, tile, chunk):
        rows = pl.ds(pl.multiple_of(chunk * self.kc, self.kc), self.kc)
        cols = pl.ds(pl.multiple_of(tile * self.tn, self.tn), self.tn)
        return pltpu.make_async_copy(self.w_hbm.at[self.widx + (rows, cols)], self.stage, self.sem)

    def _cast(self, tile, chunk):
        rows = pl.ds(pl.multiple_of(chunk * self.kc, self.kc), self.kc)
        self.wb[tile % 2, rows, :] = self.stage[...].astype(BF16)

    def begin(self):
        @pl.when((self.n == 0) & (self.m == 0))
        def _():
            def body(c, carry):
                cp = self._copy(0, c)
                cp.start()
                cp.wait()
                self._cast(0, c)
                return carry
            lax.fori_loop(0, pl.num_programs(1), body, 0)

        @pl.when(self.have_next)
        def _():
            self._copy(self.n + 1, self.m).start()

    def tile(self):
        return self.wb[self.n % 2]

    def finish(self):
        @pl.when(self.have_next)
        def _():
            self._copy(self.n + 1, self.m).wait()
            self._cast(self.n + 1, self.m)


def _stream_scratch(k, tn, mt):
    assert k % mt == 0 and (k // mt) % 16 == 0, (k, mt)
    return [pltpu.VMEM((k // mt, tn), F32), pltpu.VMEM((2, k, tn), BF16), pltpu.SemaphoreType.DMA(())]


_ANY_SPEC = pl.BlockSpec(memory_space=pl.ANY)


class _SideCast:
    def __init__(self, src_hbm, widx, dst_hbm, cin_ref, cout_ref, sem_in, sem_out):
        self.src, self.widx, self.dst = src_hbm, tuple(widx), dst_hbm
        self.cin, self.cout, self.sem_in, self.sem_out = cin_ref, cout_ref, sem_in, sem_out
        self.rps = cin_ref.shape[0]
        self.n_active = dst_hbm.shape[0] // self.rps
        self.n_steps = pl.num_programs(0) * pl.num_programs(1)
        self.s = pl.program_id(0) * pl.num_programs(1) + pl.program_id(1)

    def _rows(self, c):
        return pl.ds(pl.multiple_of(c * self.rps, self.rps), self.rps)

    def _in(self, c):
        return pltpu.make_async_copy(self.src.at[self.widx + (self._rows(c),)], self.cin, self.sem_in)

    def _out(self, c):
        return pltpu.make_async_copy(self.cout, self.dst.at[self._rows(c)], self.sem_out)

    def begin(self):
        @pl.when(self.s < self.n_active)
        def _():
            self._in(self.s).start()

    def finish(self):
        s = self.s

        @pl.when(s < self.n_active)
        def _():
            self._in(s).wait()

            @pl.when(s > 0)
            def _():
                self._out(s - 1).wait()

            self.cout[...] = self.cin[...].astype(BF16)
            self._out(s).start()

        @pl.when((s == self.n_active) | ((s == self.n_steps - 1) & (s == self.n_active - 1)))
        def _():
            self._out(self.n_active - 1).wait()


def _side_cast_rows(rows, n_steps):
    for rps in range(16, rows + 1, 16):
        if rows % rps == 0 and rows // rps <= n_steps:
            return rps
    raise ValueError((rows, n_steps))


def _side_cast_scratch(rows, cols, n_steps):
    rps = _side_cast_rows(rows, n_steps)
    return [pltpu.VMEM((rps, cols), F32), pltpu.VMEM((rps, cols), BF16),
            pltpu.SemaphoreType.DMA(()), pltpu.SemaphoreType.DMA(())]


def _mm_rope_kernel(x_ref, w_hbm, c_ref, s1_ref, s2_ref, nxt_hbm, o_ref, nxtb_hbm,
                    stage_ref, wb_ref, sem, cin_ref, cout_ref, sem_in, sem_out,
                    *, widx, tn, n_q_tiles, n_rope_tiles, q_scale):
    j = pl.program_id(0)
    ws = _WeightStream(w_hbm, widx, stage_ref, wb_ref, sem)
    side = _SideCast(nxt_hbm, widx, nxtb_hbm, cin_ref, cout_ref, sem_in, sem_out)
    ws.begin()
    side.begin()

    def matmul():
        return jnp.dot(x_ref[...], ws.tile(), preferred_element_type=F32)

    def roped(scale):
        acc = matmul()
        c, s1, s2 = c_ref[...], s1_ref[...], s2_ref[...]
        for g in range(tn // LANES):
            sl = slice(g * LANES, (g + 1) * LANES)
            y = _rope(acc[:, sl], c, s1, s2)
            o_ref[:, sl] = (y if scale is None else y * scale).astype(o_ref.dtype)

    @pl.when(j < n_q_tiles)
    def _():
        roped(q_scale)

    @pl.when((j >= n_q_tiles) & (j < n_rope_tiles))
    def _():
        roped(None)

    @pl.when(j >= n_rope_tiles)
    def _():
        o_ref[...] = matmul().astype(o_ref.dtype)

    ws.finish()
    side.finish()


def _matmul_rope(x, w, w_next, widx, tables, n_q_cols, n_rope_cols, q_scale, out_dtype=BF16):
    m, k = x.shape
    n = w.shape[-1]
    tm = _pick(m, (1024, 512, 256, 128))
    tn = _pick(n, (1024, 512, 256, 128))
    mt = m // tm
    assert n_rope_cols % tn == 0 and n_q_cols % tn == 0 and n_q_cols <= n_rope_cols
    tab_spec = pl.BlockSpec((tm, LANES), lambda j, i: (i, 0))
    kern = functools.partial(_mm_rope_kernel, widx=widx, tn=tn, n_q_tiles=n_q_cols // tn,
                             n_rope_tiles=n_rope_cols // tn, q_scale=q_scale)
    nrows, ncols = w_next.shape[-2:]
    n_steps = (n // tn) * mt
    return pl.pallas_call(
        kern,
        out_shape=[jax.ShapeDtypeStruct((m, n), out_dtype), jax.ShapeDtypeStruct((nrows, ncols), BF16)],
        grid=(n // tn, mt),
        in_specs=[pl.BlockSpec((tm, k), lambda j, i: (i, 0)), _ANY_SPEC,
                  tab_spec, tab_spec, tab_spec, _ANY_SPEC],
        out_specs=[pl.BlockSpec((tm, tn), lambda j, i: (i, j)), _ANY_SPEC],
        scratch_shapes=_stream_scratch(k, tn, mt) + _side_cast_scratch(nrows, ncols, n_steps),
        compiler_params=_cparams(2, _mm_vmem(tm, tn, k, mt, 2, extra=6 * tm * LANES * 4 + 2 * tm * tn * 4)),
        name="matmul_rope",
    )(x, w, *tables, w_next)


def _index_kw_proj(x, w_k, w_head, tables, head_scale):
    m, k = x.shape
    nh = w_head.shape[1]
    assert w_k.shape[1] == LANES and nh <= LANES
    w = jnp.concatenate([w_k, w_head, jnp.zeros((k, LANES - nh), w_k.dtype)], axis=1)
    tm = _pick(m, (1024, 512, 256, 128))
    tab_spec = pl.BlockSpec((tm, LANES), lambda i: (i, 0))
    out_spec = pl.BlockSpec((tm, LANES), lambda i: (i, 0))

    def kern(x_ref, w_ref, c_ref, s1_ref, s2_ref, ki_ref, wt_ref, wb_ref):
        @pl.when(pl.program_id(0) == 0)
        def _():
            wb_ref[...] = w_ref[...].astype(BF16)
        acc = jnp.dot(x_ref[...], wb_ref[...], preferred_element_type=F32)
        ki_ref[...] = _rope(acc[:, :LANES], c_ref[...], s1_ref[...], s2_ref[...]).astype(BF16)
        wt_ref[...] = acc[:, LANES:] * head_scale

    return pl.pallas_call(
        kern,
        out_shape=[jax.ShapeDtypeStruct((m, LANES), BF16), jax.ShapeDtypeStruct((m, LANES), F32)],
        grid=(m // tm,),
        in_specs=[pl.BlockSpec((tm, k), lambda i: (i, 0)),
                  pl.BlockSpec((k, 2 * LANES), lambda i: (0, 0)),
                  tab_spec, tab_spec, tab_spec],
        out_specs=[out_spec, out_spec],
        scratch_shapes=[pltpu.VMEM((k, 2 * LANES), BF16)],
        compiler_params=_cparams(1, 2 * tm * k * 2 + 2 * k * 2 * LANES * 4 + k * 2 * LANES * 2
                                 + 16 * tm * LANES * 4 + 4 * MIB),
        name="index_kw_proj",
    )(x, w, *tables)


def _index_q_proj(x, w, widx, tables):
    m, k = x.shape
    n = w.shape[-1]
    nh = n // LANES
    tm = _pick(m, (1024, 512, 256, 128))
    tn = _pick(n, (1024, 512, 256, 128))
    mt = m // tm
    hpt = tn // LANES
    tab_spec = pl.BlockSpec((tm, LANES), lambda j, i: (i, 0))

    def kern(x_ref, w_hbm, c_ref, s1_ref, s2_ref, o_ref, stage_ref, wb_ref, sem):
        ws = _WeightStream(w_hbm, widx, stage_ref, wb_ref, sem)
        ws.begin()
        acc = jnp.dot(x_ref[...], ws.tile(), preferred_element_type=F32)
        c, s1, s2 = c_ref[...], s1_ref[...], s2_ref[...]
        for g in range(hpt):
            o_ref[g] = _rope(acc[:, g * LANES:(g + 1) * LANES], c, s1, s2).astype(BF16)
        ws.finish()

    return pl.pallas_call(
        kern,
        out_shape=jax.ShapeDtypeStruct((nh, m, LANES), BF16),
        grid=(n // tn, mt),
        in_specs=[pl.BlockSpec((tm, k), lambda j, i: (i, 0)), _ANY_SPEC,
                  tab_spec, tab_spec, tab_spec],
        out_specs=pl.BlockSpec((hpt, tm, LANES), lambda j, i: (j, i, 0)),
        scratch_shapes=_stream_scratch(k, tn, mt),
        compiler_params=_cparams(2, _mm_vmem(tm, tn, k, mt, 2, extra=6 * tm * LANES * 4 + 2 * tm * tn * 4)),
        name="index_q_proj",
    )(x, w, *tables)


def _swiglu_kernel(x_ref, wg_hbm, wu_hbm, wd_hbm, o_ref, wdb_hbm,
                   gstage_ref, gb_ref, gsem, ustage_ref, ub_ref, usem, cin_ref, cout_ref, sem_in, sem_out,
                   *, widx):
    wgs = _WeightStream(wg_hbm, widx, gstage_ref, gb_ref, gsem)
    wus = _WeightStream(wu_hbm, widx, ustage_ref, ub_ref, usem)
    side = _SideCast(wd_hbm, widx, wdb_hbm, cin_ref, cout_ref, sem_in, sem_out)
    wgs.begin()
    wus.begin()
    side.begin()
    x = x_ref[...]
    g = jnp.dot(x, wgs.tile(), preferred_element_type=F32)
    u = jnp.dot(x, wus.tile(), preferred_element_type=F32)
    o_ref[...] = (g * jax.nn.sigmoid(g) * u).astype(o_ref.dtype)
    wgs.finish()
    wus.finish()
    side.finish()


def _swiglu_up(x, wg, wu, wd, widx):
    m, k = x.shape
    n = wg.shape[-1]
    tm = _pick(m, (1024, 512, 256, 128))
    tn = _pick(n, (512, 256, 128))
    mt = m // tm
    nrows, ncols = wd.shape[-2:]
    n_steps = (n // tn) * mt
    return pl.pallas_call(
        functools.partial(_swiglu_kernel, widx=widx),
        out_shape=[jax.ShapeDtypeStruct((m, n), BF16), jax.ShapeDtypeStruct((nrows, ncols), BF16)],
        grid=(n // tn, mt),
        in_specs=[pl.BlockSpec((tm, k), lambda j, i: (i, 0)), _ANY_SPEC, _ANY_SPEC, _ANY_SPEC],
        out_specs=[pl.BlockSpec((tm, tn), lambda j, i: (i, j)), _ANY_SPEC],
        scratch_shapes=(_stream_scratch(k, tn, mt) + _stream_scratch(k, tn, mt)
                        + _side_cast_scratch(nrows, ncols, n_steps)),
        compiler_params=_cparams(2, _mm_vmem(tm, tn, k, mt, 2, n_w=2)),
        name="swiglu_up",
    )(x, wg, wu, wd)


def _mm_residual_kernel(x_ref, w_ref, r_ref, o_ref, *, scale):
    acc = jnp.dot(x_ref[...], w_ref[...], preferred_element_type=F32)
    if scale != 1.0:
        acc = acc * scale
    o_ref[...] = r_ref[...] + acc


def _matmul_residual(x, w, res, scale):
    m, k = x.shape
    n = w.shape[1]
    tm = _pick(m, (1024, 512, 256, 128))
    tn = _pick(n, (512, 256, 128))
    vmem = 2 * tm * k * 2 + 2 * k * tn * 2 + 4 * tm * tn * 4 + 2 * tm * tn * 4 + 8 * MIB
    return pl.pallas_call(
        functools.partial(_mm_residual_kernel, scale=scale),
        out_shape=jax.ShapeDtypeStruct((m, n), F32),
        grid=(m // tm, n // tn),
        in_specs=[pl.BlockSpec((tm, k), lambda i, j: (i, 0)),
                  pl.BlockSpec((k, tn), lambda i, j: (0, j)),
                  pl.BlockSpec((tm, tn), lambda i, j: (i, j))],
        out_specs=pl.BlockSpec((tm, tn), lambda i, j: (i, j)),
        compiler_params=_cparams(2, vmem),
        name="matmul_residual",
    )(x, w, res)


ROW_BLOCK = 128
LOG2E = math.log2(math.e)
Q_PRESCALE = HEAD_DIM ** -0.5 * LOG2E


def _flash_init(m_ref, l_ref, acc_ref):
    m_ref[...] = jnp.full(m_ref.shape, NEG_INF, F32)
    l_ref[...] = jnp.zeros(l_ref.shape, F32)
    acc_ref[...] = jnp.zeros(acc_ref.shape, F32)


def _qk(q, k):
    return lax.dot_general(q, k, (((1,), (1,)), ((), ())), preferred_element_type=F32)


def _lane_tile(x, n):
    return x if n == 1 else jnp.concatenate([x] * n, axis=1)


def _softmax_update(s, m_ref):
    m_prev = m_ref[...]
    m_new = jnp.maximum(m_prev, jnp.max(s, axis=-1, keepdims=True))
    m_safe = jnp.where(m_new == NEG_INF, 0.0, m_new)
    alpha = jnp.exp2(m_prev - m_safe)
    p = jnp.exp2(s - _lane_tile(m_safe, s.shape[1] // LANES))
    m_ref[...] = m_new
    return p, alpha


def _attend(q, k, v1, bias, m_ref, l_ref, acc_ref):
    s = _qk(q, k)
    if bias is not None:
        s = s + bias
    p, alpha = _softmax_update(s, m_ref)
    pv = jnp.dot(p.astype(BF16), v1, preferred_element_type=F32)
    dv = acc_ref.shape[-1]
    l_ref[...] = alpha * l_ref[...] + pv[:, dv:]
    acc_ref[...] = alpha * acc_ref[...] + pv[:, :dv]


def _row_blocks(t):
    r = min(ROW_BLOCK, t)
    return [(rb, slice(rb * r, (rb + 1) * r), r) for rb in range(t // r)]


def _heads_per_step(n_heads, preferred):
    for c in (preferred, 2, 1):
        if c <= preferred and n_heads % c == 0:
            return c
    return 1


def _diff_attn_kernel(lam_ref, g_ref, q_ref, k_ref, v_ref, o_ref, m_ref, l_ref, acc_ref,
                      *, t, hp, lam_init):
    i = pl.program_id(2)
    dh = 2 * HEAD_DIM
    _flash_init(m_ref, l_ref, acc_ref)

    def step(off, width, rows, r, g, bias):
        v = v_ref[pl.ds(off, width), g * dh:(g + 1) * dh]
        ps, alphas = [], []
        for c in range(2):
            cols = slice(g * dh + c * HEAD_DIM, g * dh + (c + 1) * HEAD_DIM)
            s = _qk(q_ref[rows, cols], k_ref[pl.ds(off, width), cols])
            if bias is not None:
                s = s + bias
            slot = 2 * g + c
            p, alpha = _softmax_update(s, m_ref.at[slot, rows])
            l_ref[slot, rows] = alpha * l_ref[slot, rows] + jnp.sum(p, axis=-1, keepdims=True)
            ps.append(p.astype(BF16))
            alphas.append(alpha)
        pv = jnp.dot(jnp.concatenate(ps, axis=0), v, preferred_element_type=F32)
        for c in range(2):
            slot = 2 * g + c
            acc_ref[slot, rows] = (_lane_tile(alphas[c], dh // LANES) * acc_ref[slot, rows]
                                   + pv[c * r:(c + 1) * r])

    diag_off = pl.multiple_of(i * t, t)
    for rb, rows, r in _row_blocks(t):
        width = (rb + 1) * r
        row = rb * r + lax.broadcasted_iota(I32, (r, width), 0)
        col = lax.broadcasted_iota(I32, (r, width), 1)
        bias = jnp.where(col <= row, 0.0, NEG_INF)
        for g in range(hp):
            step(diag_off, width, rows, r, g, bias)

    def body(j, carry):
        off = pl.multiple_of(j * t, t)
        for rb, rows, r in _row_blocks(t):
            for g in range(hp):
                step(off, t, rows, r, g, None)
        return carry

    lax.fori_loop(0, i, body, 0)

    p = lam_ref[...]
    lam = (jnp.exp(jnp.sum(p[0:1] * p[1:2], axis=-1, keepdims=True))
           - jnp.exp(jnp.sum(p[2:3] * p[3:4], axis=-1, keepdims=True)) + lam_init)
    for g in range(hp):
        w1 = _lane_tile(1.0 / l_ref[2 * g], dh // LANES)
        w2 = _lane_tile(lam / l_ref[2 * g + 1], dh // LANES)
        o = acc_ref[2 * g] * w1 - acc_ref[2 * g + 1] * w2
        y = o * lax.rsqrt(jnp.mean(o * o, axis=-1, keepdims=True) + SUBLN_EPS) * g_ref[...]
        o_ref[:, g * dh:(g + 1) * dh] = (y * (1.0 - lam_init)).astype(o_ref.dtype)


def _diff_attention(qkv, lam_params, subln_g, lam_init, batch, seq):
    m, three_d = qkv.shape
    d = three_d // 3
    dh = 2 * HEAD_DIM
    nh = d // dh
    hp = _heads_per_step(nh, 2)
    ng = nh // hp
    w = hp * dh
    t = _pick(seq, (512, 256, 128))
    nq = seq // t
    vmem = (2 * (2 * seq * w * 2) + 4 * t * w * 2 + 2 * hp * t * (2 * LANES + dh) * 4
            + 16 * ROW_BLOCK * t * 4 + 8 * MIB)
    kern = functools.partial(_diff_attn_kernel, t=t, hp=hp, lam_init=lam_init)
    return pl.pallas_call(
        kern,
        out_shape=jax.ShapeDtypeStruct((m, d), BF16),
        grid=(batch, ng, nq),
        in_specs=[pl.BlockSpec((4, HEAD_DIM), lambda b, h, i: (0, 0)),
                  pl.BlockSpec((1, dh), lambda b, h, i: (0, 0)),
                  pl.BlockSpec((t, w), lambda b, h, i: (b * nq + i, h)),
                  pl.BlockSpec((seq, w), lambda b, h, i: (b, ng + h)),
                  pl.BlockSpec((seq, w), lambda b, h, i: (b, 2 * ng + h))],
        out_specs=pl.BlockSpec((t, w), lambda b, h, i: (b * nq + i, h)),
        scratch_shapes=[pltpu.VMEM((2 * hp, t, LANES), F32), pltpu.VMEM((2 * hp, t, LANES), F32),
                        pltpu.VMEM((2 * hp, t, dh), F32)],
        compiler_params=_cparams(3, vmem),
        name="diff_attention",
    )(lam_params, subln_g.reshape(1, dh), qkv, qkv, qkv)


def _moba_attn_kernel(q_ref, k_ref, v_ref, o_ref, kmean_ref, selb_ref, m_ref, l_ref, acc_ref,
                      *, t, hp, blk, nbp, topk):
    i = pl.program_id(2)
    seq = k_ref.shape[0]
    bpc = t // blk
    blk_shift = blk.bit_length() - 1
    heads = [slice(g * HEAD_DIM, (g + 1) * HEAD_DIM) for g in range(hp)]

    @pl.when(i == 0)
    def _():
        r = lax.broadcasted_iota(I32, (nbp, seq), 0)
        c = lax.broadcasted_iota(I32, (nbp, seq), 1)
        avg = jnp.where(c >= r * blk, jnp.where(c < (r + 1) * blk, 1.0 / blk, 0.0), 0.0).astype(BF16)
        for g, sl in enumerate(heads):
            kmean_ref[g] = jnp.dot(avg, k_ref[:, sl], preferred_element_type=F32).astype(BF16)

    n_idx = lax.broadcasted_iota(I32, (nbp, t), 0)
    n_f = n_idx.astype(F32)
    qblk = lax.shift_right_logical(i * t + lax.broadcasted_iota(I32, (nbp, t), 1), blk_shift)
    valid = n_idx < qblk
    pad = jnp.full((LANES - nbp, t), NEG_INF, F32)
    for g, sl in enumerate(heads):
        gate = _qk(kmean_ref[g], q_ref[:, sl])
        gt = jnp.where(valid, gate, NEG_INF)
        picked = jnp.zeros((nbp, t), F32)
        for _ in range(topk):
            mx = jnp.max(gt, axis=0, keepdims=True)
            first = jnp.min(jnp.where(gt == mx, n_f, float(nbp)), axis=0, keepdims=True)
            hit = n_f == first
            picked = jnp.where(hit, 1.0, picked)
            gt = jnp.where(hit, NEG_INF, gt)
        sel_t = jnp.where(valid, jnp.where(picked > 0.5, 0.0, NEG_INF), NEG_INF)
        selb_ref[g] = jnp.concatenate([sel_t, pad], axis=0).T

    _flash_init(m_ref, l_ref, acc_ref)
    ones = jnp.ones((t, LANES), BF16)

    def block_bias(g, rows, r, j, width):
        sb = selb_ref[g, rows, :]
        nr = lax.broadcasted_iota(I32, (r, LANES), 1)
        col_blk = lax.shift_right_logical(lax.broadcasted_iota(I32, (r, width), 1), blk_shift)
        out = None
        for b in range(-(-width // blk)):
            col = jnp.max(jnp.where(nr == j * bpc + b, sb, NEG_INF), axis=-1, keepdims=True)
            out = col if out is None else jnp.where(col_blk >= b, col, out)
        return out

    diag_off = pl.multiple_of(i * t, t)
    for rb, rows, r in _row_blocks(t):
        width = (rb + 1) * r
        row = rb * r + lax.broadcasted_iota(I32, (r, width), 0)
        col = lax.broadcasted_iota(I32, (r, width), 1)
        own = lax.shift_right_logical(row, blk_shift) == lax.shift_right_logical(col, blk_shift)
        for g, sl in enumerate(heads):
            bias = jnp.where(col <= row, jnp.where(own, 0.0, block_bias(g, rows, r, i, width)), NEG_INF)
            v1 = jnp.concatenate([v_ref[pl.ds(diag_off, width), sl], ones[:width]], axis=1)
            _attend(q_ref[rows, sl], k_ref[pl.ds(diag_off, width), sl], v1, bias,
                    m_ref.at[g, rows], l_ref.at[g, rows], acc_ref.at[g, rows])

    def body(j, carry):
        off = pl.multiple_of(j * t, t)
        ks = [k_ref[pl.ds(off, t), sl] for sl in heads]
        v1s = [jnp.concatenate([v_ref[pl.ds(off, t), sl], ones], axis=1) for sl in heads]
        for rb, rows, r in _row_blocks(t):
            for g, sl in enumerate(heads):
                _attend(q_ref[rows, sl], ks[g], v1s[g], block_bias(g, rows, r, j, t),
                        m_ref.at[g, rows], l_ref.at[g, rows], acc_ref.at[g, rows])
        return carry

    lax.fori_loop(0, i, body, 0)
    for g, sl in enumerate(heads):
        o_ref[:, sl] = (acc_ref[g] * (1.0 / l_ref[g])).astype(o_ref.dtype)


def _moba_attention(qkv, batch, seq):
    m, three_d = qkv.shape
    d = three_d // 3
    nh = d // HEAD_DIM
    assert seq % MOBA_BLOCK == 0 and MOBA_BLOCK & (MOBA_BLOCK - 1) == 0
    t = _pick(seq, (512, 256))
    assert t % MOBA_BLOCK == 0 and MOBA_BLOCK % ROW_BLOCK == 0
    hp = _heads_per_step(nh, 4)
    ng = nh // hp
    w = hp * HEAD_DIM
    nq = seq // t
    nb = seq // MOBA_BLOCK
    nbp = -(-nb // 16) * 16
    assert nbp <= LANES
    vmem = (2 * (2 * seq * w * 2) + 4 * t * w * 2 + hp * t * 4 * LANES * 4
            + 16 * ROW_BLOCK * t * 4 + 8 * MIB)
    kern = functools.partial(_moba_attn_kernel, t=t, hp=hp, blk=MOBA_BLOCK, nbp=nbp,
                             topk=min(MOBA_TOPK, nb))
    return pl.pallas_call(
        kern,
        out_shape=jax.ShapeDtypeStruct((m, d), BF16),
        grid=(batch, ng, nq),
        in_specs=[pl.BlockSpec((t, w), lambda b, h, i: (b * nq + i, h)),
                  pl.BlockSpec((seq, w), lambda b, h, i: (b, ng + h)),
                  pl.BlockSpec((seq, w), lambda b, h, i: (b, 2 * ng + h))],
        out_specs=pl.BlockSpec((t, w), lambda b, h, i: (b * nq + i, h)),
        scratch_shapes=[pltpu.VMEM((hp, nbp, HEAD_DIM), BF16), pltpu.VMEM((hp, t, LANES), F32),
                        pltpu.VMEM((hp, t, LANES), F32), pltpu.VMEM((hp, t, LANES), F32),
                        pltpu.VMEM((hp, t, HEAD_DIM), F32)],
        compiler_params=_cparams(3, vmem),
        name="moba_attention",
    )(qkv, qkv, qkv)


def _dsa_index_kernel(qi_ref, ki_ref, w_ref, bias_ref, key_ref, wb_ref, *, tq, tk, nh, topk):
    i = pl.program_id(1)
    w = w_ref[...]
    for h in range(nh):
        wb_ref[h] = jnp.broadcast_to(w[:, h:h + 1], (tq, LANES))

    q_all = qi_ref[...].reshape(nh * tq, HEAD_DIM)
    qpos = i * tq + lax.broadcasted_iota(I32, (tq, tk), 0)
    col = lax.broadcasted_iota(I32, (tq, tk), 1)
    n_chunks = (i * tq + tq + tk - 1) // tk

    def score_chunk(c, carry):
        off = pl.multiple_of(c * tk, tk)
        kc = ki_ref[pl.ds(off, tk), :]
        r = _qk(q_all, kc)
        acc = jnp.zeros((tq, tk), F32)
        for h in range(nh):
            wbh = wb_ref[h]
            wfull = jnp.concatenate([wbh] * (tk // LANES), axis=1)
            acc = acc + jnp.maximum(r[h * tq:(h + 1) * tq], 0.0) * wfull
        bits = lax.bitcast_convert_type(acc + 0.0, I32)
        key = bits ^ (lax.shift_right_arithmetic(bits, 31) & 0x7FFFFFFF)
        key_ref[c] = jnp.where(off + col <= qpos, key, INT_MIN)
        return carry

    lax.fori_loop(0, n_chunks, score_chunk, 0)

    def count_ge(cand):
        def add(c, part):
            ge = jnp.where(key_ref[c] >= cand, 1.0, 0.0)
            return part + sum(ge[:, g * LANES:(g + 1) * LANES] for g in range(tk // LANES))
        part = lax.fori_loop(0, n_chunks, add, jnp.zeros((tq, LANES), F32))
        return jnp.sum(part, axis=-1, keepdims=True)

    zero = jnp.zeros((tq, 1), I32)
    thr = jnp.where(count_ge(zero) >= topk, zero, INT_MIN)

    def bisect(b, thr):
        cand = thr + lax.shift_left(jnp.int32(1), 30 - b)
        return jnp.where(count_ge(cand) >= topk, cand, thr)

    thr = lax.fori_loop(0, 31, bisect, thr)
    thr = jnp.maximum(thr, INT_MIN + 1)
    has_ties = jnp.max(count_ge(thr)) > topk

    bias_ref[...] = jnp.full(bias_ref.shape, NEG_INF, bias_ref.dtype)

    @pl.when(jnp.logical_not(has_ties))
    def _():
        def emit(c, carry):
            bias_ref[c] = jnp.where(key_ref[c] >= thr, 0.0, NEG_INF).astype(bias_ref.dtype)
            return carry
        lax.fori_loop(0, n_chunks, emit, 0)

    @pl.when(has_ties)
    def _():
        need = topk - count_ge(thr + 1)
        upper = jnp.where(lax.broadcasted_iota(I32, (tk, tk), 0) <= lax.broadcasted_iota(I32, (tk, tk), 1),
                          1.0, 0.0).astype(BF16)

        def emit(c, seen):
            key = key_ref[c]
            tied = key == thr
            rank = seen + jnp.dot(jnp.where(tied, 1.0, 0.0).astype(BF16), upper,
                                  preferred_element_type=F32)
            keep = jnp.where(tied, jnp.where(rank <= need, 0.0, NEG_INF), NEG_INF)
            bias_ref[c] = jnp.where(key > thr, 0.0, keep).astype(bias_ref.dtype)
            return rank[:, tk - 1:tk]

        lax.fori_loop(0, n_chunks, emit, jnp.zeros((tq, 1), F32))


def _dsa_index_bias(qi, ki, wts, batch, seq, tk):
    nh, m, _ = qi.shape
    tq = _pick(seq, (128,))
    nq = seq // tq
    nkc = seq // tk
    topk = min(DSA_TOPK_MAX, seq // 4)
    vmem = (2 * nh * tq * HEAD_DIM * 2 + 2 * seq * HEAD_DIM * 2 + 2 * tq * LANES * 4
            + 2 * tq * seq * 2 + tq * seq * 4 + nh * tq * LANES * 4
            + 2 * nh * tq * tk * 4 + 4 * tq * seq * 4 + 6 * MIB)
    kern = functools.partial(_dsa_index_kernel, tq=tq, tk=tk, nh=nh, topk=topk)
    return pl.pallas_call(
        kern,
        out_shape=jax.ShapeDtypeStruct((batch, nkc, seq, tk), BF16),
        grid=(batch, nq),
        in_specs=[pl.BlockSpec((nh, tq, HEAD_DIM), lambda b, i: (0, b * nq + i, 0)),
                  pl.BlockSpec((seq, HEAD_DIM), lambda b, i: (b, 0)),
                  pl.BlockSpec((tq, LANES), lambda b, i: (b * nq + i, 0))],
        out_specs=pl.BlockSpec((None, nkc, tq, tk), lambda b, i: (b, 0, i, 0)),
        scratch_shapes=[pltpu.VMEM((nkc, tq, tk), I32), pltpu.VMEM((nh, tq, LANES), F32)],
        compiler_params=_cparams(2, vmem),
        name="dsa_index_bias",
    )(qi, ki, wts)


def _dsa_attn_kernel(q_ref, k_ref, v_ref, bias_ref, o_ref, m_ref, l_ref, acc_ref, *, t, hp):
    i = pl.program_id(1)
    _flash_init(m_ref, l_ref, acc_ref)
    ones = jnp.ones((t, LANES), BF16)
    heads = [slice(g * HEAD_DIM, (g + 1) * HEAD_DIM) for g in range(hp)]

    diag_off = pl.multiple_of(i * t, t)
    for rb, rows, r in _row_blocks(t):
        width = (rb + 1) * r
        bias = bias_ref[i, rows, :width].astype(F32)
        for g, sl in enumerate(heads):
            v1 = jnp.concatenate([v_ref[pl.ds(diag_off, width), sl], ones[:width]], axis=1)
            _attend(q_ref[rows, sl], k_ref[pl.ds(diag_off, width), sl], v1, bias,
                    m_ref.at[g, rows], l_ref.at[g, rows], acc_ref.at[g, rows])

    def body(j, carry):
        off = pl.multiple_of(j * t, t)
        ks = [k_ref[pl.ds(off, t), sl] for sl in heads]
        v1s = [jnp.concatenate([v_ref[pl.ds(off, t), sl], ones], axis=1) for sl in heads]
        for rb, rows, r in _row_blocks(t):
            bias = bias_ref[j, rows, :].astype(F32)
            for g, sl in enumerate(heads):
                _attend(q_ref[rows, sl], ks[g], v1s[g], bias,
                        m_ref.at[g, rows], l_ref.at[g, rows], acc_ref.at[g, rows])
        return carry

    lax.fori_loop(0, i, body, 0)
    for g, sl in enumerate(heads):
        o_ref[:, sl] = (acc_ref[g] * (1.0 / l_ref[g])).astype(o_ref.dtype)


def _dsa_attention(qkv, bias, batch, seq, t):
    m, three_d = qkv.shape
    d = three_d // 3
    nh = d // HEAD_DIM
    hp = _heads_per_step(nh, 4)
    ng = nh // hp
    w = hp * HEAD_DIM
    nq = seq // t
    nkc = seq // t
    vmem = (2 * (2 * seq * w * 2) + 4 * t * w * 2 + 2 * nkc * t * t * 2
            + hp * t * 3 * LANES * 4 + 16 * ROW_BLOCK * t * 4 + 8 * MIB)
    kern = functools.partial(_dsa_attn_kernel, t=t, hp=hp)
    return pl.pallas_call(
        kern,
        out_shape=jax.ShapeDtypeStruct((m, d), BF16),
        grid=(batch, nq, ng),
        in_specs=[pl.BlockSpec((t, w), lambda b, i, h: (b * nq + i, h)),
                  pl.BlockSpec((seq, w), lambda b, i, h: (b, ng + h)),
                  pl.BlockSpec((seq, w), lambda b, i, h: (b, 2 * ng + h)),
                  pl.BlockSpec((None, nkc, t, t), lambda b, i, h: (b, 0, i, 0))],
        out_specs=pl.BlockSpec((t, w), lambda b, i, h: (b * nq + i, h)),
        scratch_shapes=[pltpu.VMEM((hp, t, LANES), F32), pltpu.VMEM((hp, t, LANES), F32),
                        pltpu.VMEM((hp, t, HEAD_DIM), F32)],
        compiler_params=_cparams(3, vmem),
        name="dsa_attention",
    )(qkv, qkv, qkv, bias)


def _lambda_init_for(layer):
    return 0.8 - 0.6 * math.exp(-0.3 * layer)


def _ffn(h, g, w_gate, w_up, w_down, widx):
    act, w_down_bf16 = _swiglu_up(_rmsnorm(h, g, BF16), w_gate, w_up, w_down, widx)
    return _matmul_residual(act, w_down_bf16, h, 0.5)


def kernel(x, positions, norm_g, ffn_w_gate, ffn_w_up, ffn_w_down, attn_w_in, attn_w_out,
           diff_lambda_q1, diff_lambda_k1, diff_lambda_q2, diff_lambda_k2, diff_subln_g,
           idx_w_q, idx_w_k, idx_w_head, final_norm_g):
    batch, seq, d = x.shape
    depth = norm_g.shape[0]
    m = batch * seq
    tables = _rope_tables(positions)
    h = x.reshape(m, d)
    for i in range(depth):
        h = _ffn(h, norm_g[i, 0], ffn_w_gate, ffn_w_up, ffn_w_down, (i, 0))
        hn = _rmsnorm(h, norm_g[i, 1], BF16)
        qkv, w_out_bf16 = _matmul_rope(hn, attn_w_in, attn_w_out, (i,), tables, d, 2 * d, Q_PRESCALE)
        mixer = i % N_MIXERS
        j = i // N_MIXERS
        if mixer == 0:
            lam_params = jnp.stack([diff_lambda_q1[j], diff_lambda_k1[j],
                                    diff_lambda_q2[j], diff_lambda_k2[j]])
            mix = _diff_attention(qkv, lam_params, diff_subln_g[j], _lambda_init_for(i), batch, seq)
        elif mixer == 1:
            mix = _moba_attention(qkv, batch, seq)
        else:
            n_idx_heads = idx_w_head.shape[-1]
            idx_dim = idx_w_k.shape[-1]
            qi = _index_q_proj(hn, idx_w_q, (j,), tables)
            ki, wts = _index_kw_proj(hn, idx_w_k[j], idx_w_head[j], tables,
                                     n_idx_heads ** -0.5 * idx_dim ** -0.5)
            t = _pick(seq, (512, 256))
            bias = _dsa_index_bias(qi, ki, wts, batch, seq, t)
            mix = _dsa_attention(qkv, bias, batch, seq, t)
        h = _matmul_residual(mix, w_out_bf16, h, 1.0)
        h = _ffn(h, norm_g[i, 2], ffn_w_gate, ffn_w_up, ffn_w_down, (i, 1))
    return _rmsnorm(h, final_norm_g, x.dtype).reshape(batch, seq, d)
```

```python
import functools
import math

import jax
import jax.numpy as jnp
from jax import lax
from jax.experimental import pallas as pl
from jax.experimental.pallas import tpu as pltpu

F32 = jnp.float32
BF16 = jnp.bfloat16
I32 = jnp.int32

HEAD_DIM = 128
ROT_DIM = HEAD_DIM // 4
ROT_HALF = ROT_DIM // 2
ROPE_THETA = 500000.0
RMS_EPS = 1e-6
SUBLN_EPS = 1e-5
MOBA_BLOCK = 256
MOBA_TOPK = 3
DSA_TOPK_MAX = 256
N_MIXERS = 3

LANES = 128
MIB = 1024 * 1024
VMEM_CAP_BYTES = 60 * MIB
NEG_INF = float("-inf")
INT_MIN = -(2 ** 31)


def _cparams(n_axes, vmem_bytes):
    return pltpu.CompilerParams(
        dimension_semantics=("arbitrary",) * n_axes,
        vmem_limit_bytes=int(min(VMEM_CAP_BYTES, vmem_bytes)))


def _pick(n, candidates):
    for c in candidates:
        if n % c == 0:
            return c
    raise ValueError(f"no tile in {candidates} divides {n}")


def _rmsnorm_kernel(x_ref, g_ref, o_ref, *, eps):
    x = x_ref[...]
    y = x * lax.rsqrt(jnp.mean(x * x, axis=-1, keepdims=True) + eps) * g_ref[...]
    o_ref[...] = y.astype(o_ref.dtype)


def _rmsnorm(x, g, out_dtype):
    m, d = x.shape
    tm = _pick(m, (256, 128, 8))
    vmem = 2 * tm * d * (4 + jnp.dtype(out_dtype).itemsize) + 4 * tm * d * 4 + 4 * MIB
    return pl.pallas_call(
        functools.partial(_rmsnorm_kernel, eps=RMS_EPS),
        out_shape=jax.ShapeDtypeStruct((m, d), out_dtype),
        grid=(m // tm,),
        in_specs=[pl.BlockSpec((tm, d), lambda i: (i, 0)),
                  pl.BlockSpec((1, d), lambda i: (0, 0))],
        out_specs=pl.BlockSpec((tm, d), lambda i: (i, 0)),
        compiler_params=_cparams(1, vmem),
        name="rmsnorm",
    )(x, g.reshape(1, d))


def _prep_kernel(x_ref, xb_ref, ss_ref):
    x = x_ref[...]
    xb_ref[...] = x.astype(BF16)
    ss_ref[...] = jnp.broadcast_to(jnp.sum(x * x, axis=-1, keepdims=True), ss_ref.shape)


def _prep(x):
    m, d = x.shape
    tm = _pick(m, (256, 128, 8))
    return pl.pallas_call(
        _prep_kernel,
        out_shape=[jax.ShapeDtypeStruct((m, d), BF16), jax.ShapeDtypeStruct((m, LANES), F32)],
        grid=(m // tm,),
        in_specs=[pl.BlockSpec((tm, d), lambda i: (i, 0))],
        out_specs=[pl.BlockSpec((tm, d), lambda i: (i, 0)), pl.BlockSpec((tm, LANES), lambda i: (i, 0))],
        compiler_params=_cparams(1, 2 * tm * d * 6 + 4 * tm * d * 4 + 4 * MIB),
        name="prep",
    )(x)


def _row_scale(ss_ref, k, n_lanes):
    return _lane_tile(lax.rsqrt(ss_ref[...] / k + RMS_EPS), n_lanes // LANES)


def _lane_tile(x, n):
    return x if n == 1 else jnp.concatenate([x] * n, axis=1)


def _rope_table_kernel(pos_ref, inv_ref, c_ref, s1_ref, s2_ref):
    ang = pos_ref[...].astype(F32) * inv_ref[...]
    lane = lax.broadcasted_iota(I32, ang.shape, 1)
    cos = jnp.cos(ang)
    sin = jnp.sin(ang)
    c_ref[...] = jnp.where(lane < ROT_DIM, cos, 1.0)
    s1_ref[...] = jnp.where(lane < ROT_HALF, 0.0, jnp.where(lane < ROT_DIM, sin, 0.0))
    s2_ref[...] = jnp.where(lane < ROT_HALF, -sin, 0.0)


def _rope_tables(positions):
    m = positions.size
    inv = ROPE_THETA ** (-jnp.arange(0, ROT_DIM, 2, dtype=F32) / ROT_DIM)
    inv = jnp.concatenate([inv, inv, jnp.zeros((LANES - ROT_DIM,), F32)]).reshape(1, LANES)
    tm = _pick(m, (512, 256, 128, 8))
    spec = pl.BlockSpec((tm, LANES), lambda i: (i, 0))
    return pl.pallas_call(
        _rope_table_kernel,
        out_shape=[jax.ShapeDtypeStruct((m, LANES), F32)] * 3,
        grid=(m // tm,),
        in_specs=[pl.BlockSpec((tm, 1), lambda i: (i, 0)),
                  pl.BlockSpec((1, LANES), lambda i: (0, 0))],
        out_specs=[spec, spec, spec],
        compiler_params=_cparams(1, 16 * MIB),
        name="rope_tables",
    )(positions.reshape(m, 1).astype(I32), inv)


def _rope(a, c, s1, s2):
    return (a * c + pltpu.roll(a, ROT_HALF, 1) * s1
            + pltpu.roll(a, LANES - ROT_HALF, 1) * s2)


def _mm_vmem(tm, tn, k, mt, out_bytes, n_w=1, extra=0):
    return (2 * tm * k * 2 + n_w * (2 * k * tn * 2 + (k // mt) * tn * 4) + 2 * tm * tn * out_bytes
            + (n_w + 1) * tm * tn * 4 + extra + 8 * MIB)


class _WeightStream:
    def __init__(self, w_hbm, widx, stage_ref, wb_ref, sem, g_ref):
        self.w_hbm, self.widx, self.stage, self.wb, self.sem = w_hbm, tuple(widx), stage_ref, wb_ref, sem
        self.g = g_ref
        self.kc, self.tn = stage_ref.shape
        self.n = pl.program_id(0)
        self.m = pl.program_id(1)
        self.have_next = self.n + 1 < pl.num_programs(0)

    def _copy(self, tile, chunk):
        rows = pl.ds(pl.multiple_of(chunk * self.kc, self.kc), self.kc)
        cols = pl.ds(pl.multiple_of(tile * self.tn, self.tn), self.tn)
        return pltpu.make_async_copy(self.w_hbm.at[self.widx + (rows, cols)], self.stage, self.sem)

    def _cast(self, tile, chunk):
        rows = pl.ds(pl.multiple_of(chunk * self.kc, self.kc), self.kc)
        self.wb[tile % 2, rows, :] = (self.stage[...] * self.g[rows, :]).astype(BF16)

    def begin(self):
        @pl.when((self.n == 0) & (self.m == 0))
        def _():
            def body(c, carry):
                cp = self._copy(0, c)
                cp.start()
                cp.wait()
                self._cast(0, c)
                return carry
            lax.fori_loop(0, pl.num_programs(1), body, 0)

        @pl.when(self.have_next)
        def _():
            self._copy(self.n + 1, self.m).start()

    def tile(self):
        return self.wb[self.n % 2]

    def finish(self):
        @pl.when(self.have_next)
        def _():
            self._copy(self.n + 1, self.m).wait()
            self._cast(self.n + 1, self.m)


def _stream_scratch(k, tn, mt):
    assert k % mt == 0 and (k // mt) % 16 == 0, (k, mt)
    return [pltpu.VMEM((k // mt, tn), F32), pltpu.VMEM((2, k, tn), BF16), pltpu.SemaphoreType.DMA(())]


_ANY_SPEC = pl.BlockSpec(memory_space=pl.ANY)


def _gain_spec(k):
    return pl.BlockSpec((k, 1), lambda *_: (0, 0))


def _gain_vmem(k):
    return 2 * k * LANES * 4


class _SideCast:
    def __init__(self, src_hbm, widx, dst_hbm, cin_ref, cout_ref, sem_in, sem_out):
        self.src, self.widx, self.dst = src_hbm, tuple(widx), dst_hbm
        self.cin, self.cout, self.sem_in, self.sem_out = cin_ref, cout_ref, sem_in, sem_out
        self.rps = cin_ref.shape[0]
        self.n_active = dst_hbm.shape[0] // self.rps
        self.n_steps = pl.num_programs(0) * pl.num_programs(1)
        self.s = pl.program_id(0) * pl.num_programs(1) + pl.program_id(1)

    def _rows(self, c):
        return pl.ds(pl.multiple_of(c * self.rps, self.rps), self.rps)

    def _in(self, c):
        return pltpu.make_async_copy(self.src.at[self.widx + (self._rows(c),)], self.cin, self.sem_in)

    def _out(self, c):
        return pltpu.make_async_copy(self.cout, self.dst.at[self._rows(c)], self.sem_out)

    def begin(self):
        @pl.when(self.s < self.n_active)
        def _():
            self._in(self.s).start()

    def finish(self):
        s = self.s

        @pl.when(s < self.n_active)
        def _():
            self._in(s).wait()

            @pl.when(s > 0)
            def _():
                self._out(s - 1).wait()

            self.cout[...] = self.cin[...].astype(BF16)
            self._out(s).start()

        @pl.when((s == self.n_active) | ((s == self.n_steps - 1) & (s == self.n_active - 1)))
        def _():
            self._out(self.n_active - 1).wait()


def _side_cast_rows(rows, n_steps):
    for rps in range(16, rows + 1, 16):
        if rows % rps == 0 and rows // rps <= n_steps:
            return rps
    raise ValueError((rows, n_steps))


def _side_cast_scratch(rows, cols, n_steps):
    rps = _side_cast_rows(rows, n_steps)
    return [pltpu.VMEM((rps, cols), F32), pltpu.VMEM((rps, cols), BF16),
            pltpu.SemaphoreType.DMA(()), pltpu.SemaphoreType.DMA(())]


def _mm_rope_kernel(x_ref, ss_ref, g_ref, w_hbm, c_ref, s1_ref, s2_ref, nxt_hbm, o_ref, nxtb_hbm,
                    stage_ref, wb_ref, sem, cin_ref, cout_ref, sem_in, sem_out,
                    *, widx, tn, n_q_tiles, n_rope_tiles, q_scale):
    j = pl.program_id(0)
    ws = _WeightStream(w_hbm, widx, stage_ref, wb_ref, sem, g_ref)
    side = _SideCast(nxt_hbm, widx, nxtb_hbm, cin_ref, cout_ref, sem_in, sem_out)
    ws.begin()
    side.begin()

    def matmul():
        acc = jnp.dot(x_ref[...], ws.tile(), preferred_element_type=F32)
        return acc * _row_scale(ss_ref, x_ref.shape[1], tn)

    def roped(scale):
        acc = matmul()
        c, s1, s2 = c_ref[...], s1_ref[...], s2_ref[...]
        for g in range(tn // LANES):
            sl = slice(g * LANES, (g + 1) * LANES)
            y = _rope(acc[:, sl], c, s1, s2)
            o_ref[:, sl] = (y if scale is None else y * scale).astype(o_ref.dtype)

    @pl.when(j < n_q_tiles)
    def _():
        roped(q_scale)

    @pl.when((j >= n_q_tiles) & (j < n_rope_tiles))
    def _():
        roped(None)

    @pl.when(j >= n_rope_tiles)
    def _():
        o_ref[...] = matmul().astype(o_ref.dtype)

    ws.finish()
    side.finish()


def _matmul_rope(x, ss, g, w, w_next, widx, tables, n_q_cols, n_rope_cols, q_scale, out_dtype=BF16):
    m, k = x.shape
    n = w.shape[-1]
    tm = _pick(m, (1024, 512, 256, 128))
    tn = _pick(n, (1024, 512, 256, 128))
    mt = m // tm
    assert n_rope_cols % tn == 0 and n_q_cols % tn == 0 and n_q_cols <= n_rope_cols
    tab_spec = pl.BlockSpec((tm, LANES), lambda j, i: (i, 0))
    kern = functools.partial(_mm_rope_kernel, widx=widx, tn=tn, n_q_tiles=n_q_cols // tn,
                             n_rope_tiles=n_rope_cols // tn, q_scale=q_scale)
    nrows, ncols = w_next.shape[-2:]
    n_steps = (n // tn) * mt
    return pl.pallas_call(
        kern,
        out_shape=[jax.ShapeDtypeStruct((m, n), out_dtype), jax.ShapeDtypeStruct((nrows, ncols), BF16)],
        grid=(n // tn, mt),
        in_specs=[pl.BlockSpec((tm, k), lambda j, i: (i, 0)), tab_spec, _gain_spec(k), _ANY_SPEC,
                  tab_spec, tab_spec, tab_spec, _ANY_SPEC],
        out_specs=[pl.BlockSpec((tm, tn), lambda j, i: (i, j)), _ANY_SPEC],
        scratch_shapes=_stream_scratch(k, tn, mt) + _side_cast_scratch(nrows, ncols, n_steps),
        compiler_params=_cparams(2, _mm_vmem(tm, tn, k, mt, 2, extra=8 * tm * LANES * 4 + 2 * tm * tn * 4
                                             + _gain_vmem(k))),
        name="matmul_rope",
    )(x, ss, g.reshape(k, 1), w, *tables, w_next)


def _index_kw_proj(x, ss, g, w_k, w_head, tables, head_scale):
    m, k = x.shape
    nh = w_head.shape[1]
    assert w_k.shape[1] == LANES and nh <= LANES
    w = jnp.concatenate([w_k, w_head, jnp.zeros((k, LANES - nh), w_k.dtype)], axis=1)
    tm = _pick(m, (1024, 512, 256, 128))
    tab_spec = pl.BlockSpec((tm, LANES), lambda i: (i, 0))
    out_spec = pl.BlockSpec((tm, LANES), lambda i: (i, 0))

    def kern(x_ref, ss_ref, g_ref, w_ref, c_ref, s1_ref, s2_ref, ki_ref, wt_ref, wb_ref):
        @pl.when(pl.program_id(0) == 0)
        def _():
            wb_ref[...] = (w_ref[...] * g_ref[...]).astype(BF16)
        acc = jnp.dot(x_ref[...], wb_ref[...], preferred_element_type=F32) * _row_scale(ss_ref, k, 2 * LANES)
        ki_ref[...] = _rope(acc[:, :LANES], c_ref[...], s1_ref[...], s2_ref[...]).astype(BF16)
        wt_ref[...] = acc[:, LANES:] * head_scale

    return pl.pallas_call(
        kern,
        out_shape=[jax.ShapeDtypeStruct((m, LANES), BF16), jax.ShapeDtypeStruct((m, LANES), F32)],
        grid=(m // tm,),
        in_specs=[pl.BlockSpec((tm, k), lambda i: (i, 0)), tab_spec, _gain_spec(k),
                  pl.BlockSpec((k, 2 * LANES), lambda i: (0, 0)),
                  tab_spec, tab_spec, tab_spec],
        out_specs=[out_spec, out_spec],
        scratch_shapes=[pltpu.VMEM((k, 2 * LANES), BF16)],
        compiler_params=_cparams(1, 2 * tm * k * 2 + 2 * k * 2 * LANES * 4 + k * 2 * LANES * 2
                                 + 18 * tm * LANES * 4 + _gain_vmem(k) + 4 * MIB),
        name="index_kw_proj",
    )(x, ss, g.reshape(k, 1), w, *tables)


def _index_q_proj(x, ss, g, w, widx, tables):
    m, k = x.shape
    n = w.shape[-1]
    nh = n // LANES
    tm = _pick(m, (1024, 512, 256, 128))
    tn = _pick(n, (1024, 512, 256, 128))
    mt = m // tm
    hpt = tn // LANES
    tab_spec = pl.BlockSpec((tm, LANES), lambda j, i: (i, 0))

    def kern(x_ref, ss_ref, g_ref, w_hbm, c_ref, s1_ref, s2_ref, o_ref, stage_ref, wb_ref, sem):
        ws = _WeightStream(w_hbm, widx, stage_ref, wb_ref, sem, g_ref)
        ws.begin()
        acc = jnp.dot(x_ref[...], ws.tile(), preferred_element_type=F32) * _row_scale(ss_ref, k, tn)
        c, s1, s2 = c_ref[...], s1_ref[...], s2_ref[...]
        for g in range(hpt):
            o_ref[g] = _rope(acc[:, g * LANES:(g + 1) * LANES], c, s1, s2).astype(BF16)
        ws.finish()

    return pl.pallas_call(
        kern,
        out_shape=jax.ShapeDtypeStruct((nh, m, LANES), BF16),
        grid=(n // tn, mt),
        in_specs=[pl.BlockSpec((tm, k), lambda j, i: (i, 0)), tab_spec, _gain_spec(k), _ANY_SPEC,
                  tab_spec, tab_spec, tab_spec],
        out_specs=pl.BlockSpec((hpt, tm, LANES), lambda j, i: (j, i, 0)),
        scratch_shapes=_stream_scratch(k, tn, mt),
        compiler_params=_cparams(2, _mm_vmem(tm, tn, k, mt, 2, extra=8 * tm * LANES * 4 + 2 * tm * tn * 4
                                             + _gain_vmem(k))),
        name="index_q_proj",
    )(x, ss, g.reshape(k, 1), w, *tables)


def _swiglu_kernel(x_ref, ss_ref, g_ref, wg_hbm, wu_hbm, wd_hbm, o_ref, wdb_hbm,
                   gstage_ref, gb_ref, gsem, ustage_ref, ub_ref, usem, cin_ref, cout_ref, sem_in, sem_out,
                   *, widx):
    wgs = _WeightStream(wg_hbm, widx, gstage_ref, gb_ref, gsem, g_ref)
    wus = _WeightStream(wu_hbm, widx, ustage_ref, ub_ref, usem, g_ref)
    side = _SideCast(wd_hbm, widx, wdb_hbm, cin_ref, cout_ref, sem_in, sem_out)
    wgs.begin()
    wus.begin()
    side.begin()
    x = x_ref[...]
    r = _row_scale(ss_ref, x.shape[1], o_ref.shape[1])
    g = jnp.dot(x, wgs.tile(), preferred_element_type=F32) * r
    u = jnp.dot(x, wus.tile(), preferred_element_type=F32) * r
    o_ref[...] = (g * jax.nn.sigmoid(g) * u).astype(o_ref.dtype)
    wgs.finish()
    wus.finish()
    side.finish()


def _swiglu_up(x, ss, g, wg, wu, wd, widx):
    m, k = x.shape
    n = wg.shape[-1]
    tm = _pick(m, (1024, 512, 256, 128))
    tn = _pick(n, (512, 256, 128))
    mt = m // tm
    nrows, ncols = wd.shape[-2:]
    n_steps = (n // tn) * mt
    return pl.pallas_call(
        functools.partial(_swiglu_kernel, widx=widx),
        out_shape=[jax.ShapeDtypeStruct((m, n), BF16), jax.ShapeDtypeStruct((nrows, ncols), BF16)],
        grid=(n // tn, mt),
        in_specs=[pl.BlockSpec((tm, k), lambda j, i: (i, 0)),
                  pl.BlockSpec((tm, LANES), lambda j, i: (i, 0)), _gain_spec(k),
                  _ANY_SPEC, _ANY_SPEC, _ANY_SPEC],
        out_specs=[pl.BlockSpec((tm, tn), lambda j, i: (i, j)), _ANY_SPEC],
        scratch_shapes=(_stream_scratch(k, tn, mt) + _stream_scratch(k, tn, mt)
                        + _side_cast_scratch(nrows, ncols, n_steps)),
        compiler_params=_cparams(2, _mm_vmem(tm, tn, k, mt, 2, n_w=2,
                                             extra=2 * tm * LANES * 4 + _gain_vmem(k))),
        name="swiglu_up",
    )(x, ss, g.reshape(k, 1), wg, wu, wd)


def _mm_residual_kernel(x_ref, w_ref, r_ref, o_ref, ob_ref, ss_ref, *, scale):
    acc = jnp.dot(x_ref[...], w_ref[...], preferred_element_type=F32)
    if scale != 1.0:
        acc = acc * scale
    o = r_ref[...] + acc
    o_ref[...] = o
    ob_ref[...] = o.astype(BF16)
    part = jnp.broadcast_to(jnp.sum(o * o, axis=-1, keepdims=True), ss_ref.shape)

    @pl.when(pl.program_id(1) == 0)
    def _():
        ss_ref[...] = part

    @pl.when(pl.program_id(1) > 0)
    def _():
        ss_ref[...] = ss_ref[...] + part


def _matmul_residual(x, w, res, scale):
    m, k = x.shape
    n = w.shape[1]
    tm = _pick(m, (1024, 512, 256, 128))
    tn = _pick(n, (512, 256, 128))
    vmem = (2 * tm * k * 2 + 2 * k * tn * 2 + 4 * tm * tn * 4 + 2 * tm * tn * 2 + 3 * tm * tn * 4
            + 2 * tm * LANES * 4 + 8 * MIB)
    tile = pl.BlockSpec((tm, tn), lambda i, j: (i, j))
    return pl.pallas_call(
        functools.partial(_mm_residual_kernel, scale=scale),
        out_shape=[jax.ShapeDtypeStruct((m, n), F32), jax.ShapeDtypeStruct((m, n), BF16),
                   jax.ShapeDtypeStruct((m, LANES), F32)],
        grid=(m // tm, n // tn),
        in_specs=[pl.BlockSpec((tm, k), lambda i, j: (i, 0)),
                  pl.BlockSpec((k, tn), lambda i, j: (0, j)),
                  tile],
        out_specs=[tile, tile, pl.BlockSpec((tm, LANES), lambda i, j: (i, 0))],
        compiler_params=_cparams(2, vmem),
        name="matmul_residual",
    )(x, w, res)


ROW_BLOCK = 128
LOG2E = math.log2(math.e)
Q_PRESCALE = HEAD_DIM ** -0.5 * LOG2E


def _flash_init(m_ref, l_ref, acc_ref):
    m_ref[...] = jnp.full(m_ref.shape, NEG_INF, F32)
    l_ref[...] = jnp.zeros(l_ref.shape, F32)
    acc_ref[...] = jnp.zeros(acc_ref.shape, F32)


def _qk(q, k):
    return lax.dot_general(q, k, (((1,), (1,)), ((), ())), preferred_element_type=F32)


def _softmax_update(s, m_ref):
    m_prev = m_ref[...]
    m_new = jnp.maximum(m_prev, jnp.max(s, axis=-1, keepdims=True))
    m_safe = jnp.where(m_new == NEG_INF, 0.0, m_new)
    alpha = jnp.exp2(m_prev - m_safe)
    p = jnp.exp2(s - _lane_tile(m_safe, s.shape[1] // LANES))
    m_ref[...] = m_new
    return p, alpha


def _attend(q, k, v1, bias, m_ref, l_ref, acc_ref):
    s = _qk(q, k)
    if bias is not None:
        s = s + bias
    p, alpha = _softmax_update(s, m_ref)
    pv = jnp.dot(p.astype(BF16), v1, preferred_element_type=F32)
    dv = acc_ref.shape[-1]
    l_ref[...] = alpha * l_ref[...] + pv[:, dv:]
    acc_ref[...] = alpha * acc_ref[...] + pv[:, :dv]


def _row_blocks(t):
    r = min(ROW_BLOCK, t)
    return [(rb, slice(rb * r, (rb + 1) * r), r) for rb in range(t // r)]


def _heads_per_step(n_heads, preferred):
    for c in (preferred, 2, 1):
        if c <= preferred and n_heads % c == 0:
            return c
    return 1


def _diff_attn_kernel(lam_ref, g_ref, q_ref, k_ref, v_ref, o_ref, m_ref, l_ref, acc_ref,
                      *, t, hp, lam_init):
    i = pl.program_id(2)
    dh = 2 * HEAD_DIM
    _flash_init(m_ref, l_ref, acc_ref)

    def step(off, width, rows, r, g, bias):
        v = v_ref[pl.ds(off, width), g * dh:(g + 1) * dh]
        ps, alphas = [], []
        for c in range(2):
            cols = slice(g * dh + c * HEAD_DIM, g * dh + (c + 1) * HEAD_DIM)
            s = _qk(q_ref[rows, cols], k_ref[pl.ds(off, width), cols])
            if bias is not None:
                s = s + bias
            slot = 2 * g + c
            p, alpha = _softmax_update(s, m_ref.at[slot, rows])
            l_ref[slot, rows] = alpha * l_ref[slot, rows] + jnp.sum(p, axis=-1, keepdims=True)
            ps.append(p.astype(BF16))
            alphas.append(alpha)
        pv = jnp.dot(jnp.concatenate(ps, axis=0), v, preferred_element_type=F32)
        for c in range(2):
            slot = 2 * g + c
            acc_ref[slot, rows] = (_lane_tile(alphas[c], dh // LANES) * acc_ref[slot, rows]
                                   + pv[c * r:(c + 1) * r])

    diag_off = pl.multiple_of(i * t, t)
    for rb, rows, r in _row_blocks(t):
        width = (rb + 1) * r
        row = rb * r + lax.broadcasted_iota(I32, (r, width), 0)
        col = lax.broadcasted_iota(I32, (r, width), 1)
        bias = jnp.where(col <= row, 0.0, NEG_INF)
        for g in range(hp):
            step(diag_off, width, rows, r, g, bias)

    def body(j, carry):
        off = pl.multiple_of(j * t, t)
        for rb, rows, r in _row_blocks(t):
            for g in range(hp):
                step(off, t, rows, r, g, None)
        return carry

    lax.fori_loop(0, i, body, 0)

    p = lam_ref[...]
    lam = (jnp.exp(jnp.sum(p[0:1] * p[1:2], axis=-1, keepdims=True))
           - jnp.exp(jnp.sum(p[2:3] * p[3:4], axis=-1, keepdims=True)) + lam_init)
    for g in range(hp):
        w1 = _lane_tile(1.0 / l_ref[2 * g], dh // LANES)
        w2 = _lane_tile(lam / l_ref[2 * g + 1], dh // LANES)
        o = acc_ref[2 * g] * w1 - acc_ref[2 * g + 1] * w2
        y = o * lax.rsqrt(jnp.mean(o * o, axis=-1, keepdims=True) + SUBLN_EPS) * g_ref[...]
        o_ref[:, g * dh:(g + 1) * dh] = (y * (1.0 - lam_init)).astype(o_ref.dtype)


def _diff_attention(qkv, lam_params, subln_g, lam_init, batch, seq):
    m, three_d = qkv.shape
    d = three_d // 3
    dh = 2 * HEAD_DIM
    nh = d // dh
    hp = _heads_per_step(nh, 2)
    ng = nh // hp
    w = hp * dh
    t = _pick(seq, (512, 256, 128))
    nq = seq // t
    vmem = (2 * (2 * seq * w * 2) + 4 * t * w * 2 + 2 * hp * t * (2 * LANES + dh) * 4
            + 16 * ROW_BLOCK * t * 4 + 8 * MIB)
    kern = functools.partial(_diff_attn_kernel, t=t, hp=hp, lam_init=lam_init)
    return pl.pallas_call(
        kern,
        out_shape=jax.ShapeDtypeStruct((m, d), BF16),
        grid=(batch, ng, nq),
        in_specs=[pl.BlockSpec((4, HEAD_DIM), lambda b, h, i: (0, 0)),
                  pl.BlockSpec((1, dh), lambda b, h, i: (0, 0)),
                  pl.BlockSpec((t, w), lambda b, h, i: (b * nq + i, h)),
                  pl.BlockSpec((seq, w), lambda b, h, i: (b, ng + h)),
                  pl.BlockSpec((seq, w), lambda b, h, i: (b, 2 * ng + h))],
        out_specs=pl.BlockSpec((t, w), lambda b, h, i: (b * nq + i, h)),
        scratch_shapes=[pltpu.VMEM((2 * hp, t, LANES), F32), pltpu.VMEM((2 * hp, t, LANES), F32),
                        pltpu.VMEM((2 * hp, t, dh), F32)],
        compiler_params=_cparams(3, vmem),
        name="diff_attention",
    )(lam_params, subln_g.reshape(1, dh), qkv, qkv, qkv)


def _moba_attn_kernel(q_ref, k_ref, v_ref, o_ref, kmean_ref, selb_ref, m_ref, l_ref, acc_ref,
                      *, t, hp, blk, nbp, topk):
    i = pl.program_id(2)
    seq = k_ref.shape[0]
    bpc = t // blk
    blk_shift = blk.bit_length() - 1
    heads = [slice(g * HEAD_DIM, (g + 1) * HEAD_DIM) for g in range(hp)]

    @pl.when(i == 0)
    def _():
        r = lax.broadcasted_iota(I32, (nbp, seq), 0)
        c = lax.broadcasted_iota(I32, (nbp, seq), 1)
        avg = jnp.where(c >= r * blk, jnp.where(c < (r + 1) * blk, 1.0 / blk, 0.0), 0.0).astype(BF16)
        for g, sl in enumerate(heads):
            kmean_ref[g] = jnp.dot(avg, k_ref[:, sl], preferred_element_type=F32).astype(BF16)

    n_idx = lax.broadcasted_iota(I32, (nbp, t), 0)
    n_f = n_idx.astype(F32)
    qblk = lax.shift_right_logical(i * t + lax.broadcasted_iota(I32, (nbp, t), 1), blk_shift)
    valid = n_idx < qblk
    pad = jnp.full((LANES - nbp, t), NEG_INF, F32)
    for g, sl in enumerate(heads):
        gate = _qk(kmean_ref[g], q_ref[:, sl])
        gt = jnp.where(valid, gate, NEG_INF)
        picked = jnp.zeros((nbp, t), F32)
        for _ in range(topk):
            mx = jnp.max(gt, axis=0, keepdims=True)
            first = jnp.min(jnp.where(gt == mx, n_f, float(nbp)), axis=0, keepdims=True)
            hit = n_f == first
            picked = jnp.where(hit, 1.0, picked)
            gt = jnp.where(hit, NEG_INF, gt)
        sel_t = jnp.where(valid, jnp.where(picked > 0.5, 0.0, NEG_INF), NEG_INF)
        selb_ref[g] = jnp.concatenate([sel_t, pad], axis=0).T

    _flash_init(m_ref, l_ref, acc_ref)
    ones = jnp.ones((t, LANES), BF16)

    def block_bias(g, rows, r, j, width):
        sb = selb_ref[g, rows, :]
        nr = lax.broadcasted_iota(I32, (r, LANES), 1)
        col_blk = lax.shift_right_logical(lax.broadcasted_iota(I32, (r, width), 1), blk_shift)
        out = None
        for b in range(-(-width // blk)):
            col = jnp.max(jnp.where(nr == j * bpc + b, sb, NEG_INF), axis=-1, keepdims=True)
            out = col if out is None else jnp.where(col_blk >= b, col, out)
        return out

    diag_off = pl.multiple_of(i * t, t)
    for rb, rows, r in _row_blocks(t):
        width = (rb + 1) * r
        row = rb * r + lax.broadcasted_iota(I32, (r, width), 0)
        col = lax.broadcasted_iota(I32, (r, width), 1)
        own = lax.shift_right_logical(row, blk_shift) == lax.shift_right_logical(col, blk_shift)
        for g, sl in enumerate(heads):
            bias = jnp.where(col <= row, jnp.where(own, 0.0, block_bias(g, rows, r, i, width)), NEG_INF)
            v1 = jnp.concatenate([v_ref[pl.ds(diag_off, width), sl], ones[:width]], axis=1)
            _attend(q_ref[rows, sl], k_ref[pl.ds(diag_off, width), sl], v1, bias,
                    m_ref.at[g, rows], l_ref.at[g, rows], acc_ref.at[g, rows])

    def body(j, carry):
        off = pl.multiple_of(j * t, t)
        ks = [k_ref[pl.ds(off, t), sl] for sl in heads]
        v1s = [jnp.concatenate([v_ref[pl.ds(off, t), sl], ones], axis=1) for sl in heads]
        for rb, rows, r in _row_blocks(t):
            for g, sl in enumerate(heads):
                _attend(q_ref[rows, sl], ks[g], v1s[g], block_bias(g, rows, r, j, t),
                        m_ref.at[g, rows], l_ref.at[g, rows], acc_ref.at[g, rows])
        return carry

    lax.fori_loop(0, i, body, 0)
    for g, sl in enumerate(heads):
        o_ref[:, sl] = (acc_ref[g] * (1.0 / l_ref[g])).astype(o_ref.dtype)


def _moba_attention(qkv, batch, seq):
    m, three_d = qkv.shape
    d = three_d // 3
    nh = d // HEAD_DIM
    assert seq % MOBA_BLOCK == 0 and MOBA_BLOCK & (MOBA_BLOCK - 1) == 0
    t = _pick(seq, (512, 256))
    assert t % MOBA_BLOCK == 0 and MOBA_BLOCK % ROW_BLOCK == 0
    hp = _heads_per_step(nh, 4)
    ng = nh // hp
    w = hp * HEAD_DIM
    nq = seq // t
    nb = seq // MOBA_BLOCK
    nbp = -(-nb // 16) * 16
    assert nbp <= LANES
    vmem = (2 * (2 * seq * w * 2) + 4 * t * w * 2 + hp * t * 4 * LANES * 4
            + 16 * ROW_BLOCK * t * 4 + 8 * MIB)
    kern = functools.partial(_moba_attn_kernel, t=t, hp=hp, blk=MOBA_BLOCK, nbp=nbp,
                             topk=min(MOBA_TOPK, nb))
    return pl.pallas_call(
        kern,
        out_shape=jax.ShapeDtypeStruct((m, d), BF16),
        grid=(batch, ng, nq),
        in_specs=[pl.BlockSpec((t, w), lambda b, h, i: (b * nq + i, h)),
                  pl.BlockSpec((seq, w), lambda b, h, i: (b, ng + h)),
                  pl.BlockSpec((seq, w), lambda b, h, i: (b, 2 * ng + h))],
        out_specs=pl.BlockSpec((t, w), lambda b, h, i: (b * nq + i, h)),
        scratch_shapes=[pltpu.VMEM((hp, nbp, HEAD_DIM), BF16), pltpu.VMEM((hp, t, LANES), F32),
                        pltpu.VMEM((hp, t, LANES), F32), pltpu.VMEM((hp, t, LANES), F32),
                        pltpu.VMEM((hp, t, HEAD_DIM), F32)],
        compiler_params=_cparams(3, vmem),
        name="moba_attention",
    )(qkv, qkv, qkv)


def _dsa_index_kernel(qi_ref, ki_ref, w_ref, bias_ref, key_ref, wb_ref, *, tq, tk, nh, topk):
    i = pl.program_id(1)
    w = w_ref[...]
    for h in range(nh):
        wb_ref[h] = jnp.broadcast_to(w[:, h:h + 1], (tq, LANES))

    q_all = qi_ref[...].reshape(nh * tq, HEAD_DIM)
    qpos = i * tq + lax.broadcasted_iota(I32, (tq, tk), 0)
    col = lax.broadcasted_iota(I32, (tq, tk), 1)
    n_chunks = (i * tq + tq + tk - 1) // tk

    def score_chunk(c, carry):
        off = pl.multiple_of(c * tk, tk)
        kc = ki_ref[pl.ds(off, tk), :]
        r = _qk(q_all, kc)
        acc = jnp.zeros((tq, tk), F32)
        for h in range(nh):
            wbh = wb_ref[h]
            wfull = jnp.concatenate([wbh] * (tk // LANES), axis=1)
            acc = acc + jnp.maximum(r[h * tq:(h + 1) * tq], 0.0) * wfull
        bits = lax.bitcast_convert_type(acc + 0.0, I32)
        key = bits ^ (lax.shift_right_arithmetic(bits, 31) & 0x7FFFFFFF)
        key_ref[c] = jnp.where(off + col <= qpos, key, INT_MIN)
        return carry

    lax.fori_loop(0, n_chunks, score_chunk, 0)

    def count_ge(cand):
        def add(c, part):
            ge = jnp.where(key_ref[c] >= cand, 1.0, 0.0)
            return part + sum(ge[:, g * LANES:(g + 1) * LANES] for g in range(tk // LANES))
        part = lax.fori_loop(0, n_chunks, add, jnp.zeros((tq, LANES), F32))
        return jnp.sum(part, axis=-1, keepdims=True)

    zero = jnp.zeros((tq, 1), I32)
    thr = jnp.where(count_ge(zero) >= topk, zero, INT_MIN)

    def bisect(b, thr):
        cand = thr + lax.shift_left(jnp.int32(1), 30 - b)
        return jnp.where(count_ge(cand) >= topk, cand, thr)

    thr = lax.fori_loop(0, 31, bisect, thr)
    thr = jnp.maximum(thr, INT_MIN + 1)
    has_ties = jnp.max(count_ge(thr)) > topk

    bias_ref[...] = jnp.full(bias_ref.shape, NEG_INF, bias_ref.dtype)

    @pl.when(jnp.logical_not(has_ties))
    def _():
        def emit(c, carry):
            bias_ref[c] = jnp.where(key_ref[c] >= thr, 0.0, NEG_INF).astype(bias_ref.dtype)
            return carry
        lax.fori_loop(0, n_chunks, emit, 0)

    @pl.when(has_ties)
    def _():
        need = topk - count_ge(thr + 1)
        upper = jnp.where(lax.broadcasted_iota(I32, (tk, tk), 0) <= lax.broadcasted_iota(I32, (tk, tk), 1),
                          1.0, 0.0).astype(BF16)

        def emit(c, seen):
            key = key_ref[c]
            tied = key == thr
            rank = seen + jnp.dot(jnp.where(tied, 1.0, 0.0).astype(BF16), upper,
                                  preferred_element_type=F32)
            keep = jnp.where(tied, jnp.where(rank <= need, 0.0, NEG_INF), NEG_INF)
            bias_ref[c] = jnp.where(key > thr, 0.0, keep).astype(bias_ref.dtype)
            return rank[:, tk - 1:tk]

        lax.fori_loop(0, n_chunks, emit, jnp.zeros((tq, 1), F32))


def _dsa_index_bias(qi, ki, wts, batch, seq, tk):
    nh, m, _ = qi.shape
    tq = _pick(seq, (128,))
    nq = seq // tq
    nkc = seq // tk
    topk = min(DSA_TOPK_MAX, seq // 4)
    vmem = (2 * nh * tq * HEAD_DIM * 2 + 2 * seq * HEAD_DIM * 2 + 2 * tq * LANES * 4
            + 2 * tq * seq * 2 + tq * seq * 4 + nh * tq * LANES * 4
            + 2 * nh * tq * tk * 4 + 4 * tq * seq * 4 + 6 * MIB)
    kern = functools.partial(_dsa_index_kernel, tq=tq, tk=tk, nh=nh, topk=topk)
    return pl.pallas_call(
        kern,
        out_shape=jax.ShapeDtypeStruct((batch, nkc, seq, tk), BF16),
        grid=(batch, nq),
        in_specs=[pl.BlockSpec((nh, tq, HEAD_DIM), lambda b, i: (0, b * nq + i, 0)),
                  pl.BlockSpec((seq, HEAD_DIM), lambda b, i: (b, 0)),
                  pl.BlockSpec((tq, LANES), lambda b, i: (b * nq + i, 0))],
        out_specs=pl.BlockSpec((None, nkc, tq, tk), lambda b, i: (b, 0, i, 0)),
        scratch_shapes=[pltpu.VMEM((nkc, tq, tk), I32), pltpu.VMEM((nh, tq, LANES), F32)],
        compiler_params=_cparams(2, vmem),
        name="dsa_index_bias",
    )(qi, ki, wts)


def _dsa_attn_kernel(q_ref, k_ref, v_ref, bias_ref, o_ref, m_ref, l_ref, acc_ref, *, t, hp):
    i = pl.program_id(1)
    _flash_init(m_ref, l_ref, acc_ref)
    ones = jnp.ones((t, LANES), BF16)
    heads = [slice(g * HEAD_DIM, (g + 1) * HEAD_DIM) for g in range(hp)]

    diag_off = pl.multiple_of(i * t, t)
    for rb, rows, r in _row_blocks(t):
        width = (rb + 1) * r
        bias = bias_ref[i, rows, :width].astype(F32)
        for g, sl in enumerate(heads):
            v1 = jnp.concatenate([v_ref[pl.ds(diag_off, width), sl], ones[:width]], axis=1)
            _attend(q_ref[rows, sl], k_ref[pl.ds(diag_off, width), sl], v1, bias,
                    m_ref.at[g, rows], l_ref.at[g, rows], acc_ref.at[g, rows])

    def body(j, carry):
        off = pl.multiple_of(j * t, t)
        ks = [k_ref[pl.ds(off, t), sl] for sl in heads]
        v1s = [jnp.concatenate([v_ref[pl.ds(off, t), sl], ones], axis=1) for sl in heads]
        for rb, rows, r in _row_blocks(t):
            bias = bias_ref[j, rows, :].astype(F32)
            for g, sl in enumerate(heads):
                _attend(q_ref[rows, sl], ks[g], v1s[g], bias,
                        m_ref.at[g, rows], l_ref.at[g, rows], acc_ref.at[g, rows])
        return carry

    lax.fori_loop(0, i, body, 0)
    for g, sl in enumerate(heads):
        o_ref[:, sl] = (acc_ref[g] * (1.0 / l_ref[g])).astype(o_ref.dtype)


def _dsa_attention(qkv, bias, batch, seq, t):
    m, three_d = qkv.shape
    d = three_d // 3
    nh = d // HEAD_DIM
    hp = _heads_per_step(nh, 4)
    ng = nh // hp
    w = hp * HEAD_DIM
    nq = seq // t
    nkc = seq // t
    vmem = (2 * (2 * seq * w * 2) + 4 * t * w * 2 + 2 * nkc * t * t * 2
            + hp * t * 3 * LANES * 4 + 16 * ROW_BLOCK * t * 4 + 8 * MIB)
    kern = functools.partial(_dsa_attn_kernel, t=t, hp=hp)
    return pl.pallas_call(
        kern,
        out_shape=jax.ShapeDtypeStruct((m, d), BF16),
        grid=(batch, nq, ng),
        in_specs=[pl.BlockSpec((t, w), lambda b, i, h: (b * nq + i, h)),
                  pl.BlockSpec((seq, w), lambda b, i, h: (b, ng + h)),
                  pl.BlockSpec((seq, w), lambda b, i, h: (b, 2 * ng + h)),
                  pl.BlockSpec((None, nkc, t, t), lambda b, i, h: (b, 0, i, 0))],
        out_specs=pl.BlockSpec((t, w), lambda b, i, h: (b * nq + i, h)),
        scratch_shapes=[pltpu.VMEM((hp, t, LANES), F32), pltpu.VMEM((hp, t, LANES), F32),
                        pltpu.VMEM((hp, t, HEAD_DIM), F32)],
        compiler_params=_cparams(3, vmem),
        name="dsa_attention",
    )(qkv, qkv, qkv, bias)


def _lambda_init_for(layer):
    return 0.8 - 0.6 * math.exp(-0.3 * layer)


def _ffn(h, hb, ss, g, w_gate, w_up, w_down, widx):
    act, w_down_bf16 = _swiglu_up(hb, ss, g, w_gate, w_up, w_down, widx)
    return _matmul_residual(act, w_down_bf16, h, 0.5)


def kernel(x, positions, norm_g, ffn_w_gate, ffn_w_up, ffn_w_down, attn_w_in, attn_w_out,
           diff_lambda_q1, diff_lambda_k1, diff_lambda_q2, diff_lambda_k2, diff_subln_g,
           idx_w_q, idx_w_k, idx_w_head, final_norm_g):
    batch, seq, d = x.shape
    depth = norm_g.shape[0]
    m = batch * seq
    tables = _rope_tables(positions)
    h = x.reshape(m, d)
    hb, ss = _prep(h)
    for i in range(depth):
        h, hb, ss = _ffn(h, hb, ss, norm_g[i, 0], ffn_w_gate, ffn_w_up, ffn_w_down, (i, 0))
        qkv, w_out_bf16 = _matmul_rope(hb, ss, norm_g[i, 1], attn_w_in, attn_w_out, (i,), tables,
                                       d, 2 * d, Q_PRESCALE)
        mixer = i % N_MIXERS
        j = i // N_MIXERS
        if mixer == 0:
            lam_params = jnp.stack([diff_lambda_q1[j], diff_lambda_k1[j],
                                    diff_lambda_q2[j], diff_lambda_k2[j]])
            mix = _diff_attention(qkv, lam_params, diff_subln_g[j], _lambda_init_for(i), batch, seq)
        elif mixer == 1:
            mix = _moba_attention(qkv, batch, seq)
        else:
            n_idx_heads = idx_w_head.shape[-1]
            idx_dim = idx_w_k.shape[-1]
            qi = _index_q_proj(hb, ss, norm_g[i, 1], idx_w_q, (j,), tables)
            ki, wts = _index_kw_proj(hb, ss, norm_g[i, 1], idx_w_k[j], idx_w_head[j], tables,
                                     n_idx_heads ** -0.5 * idx_dim ** -0.5)
            t = _pick(seq, (512, 256))
            bias = _dsa_index_bias(qi, ki, wts, batch, seq, t)
            mix = _dsa_attention(qkv, bias, batch, seq, t)
        h, hb, ss = _matmul_residual(mix, w_out_bf16, h, 1.0)
        h, hb, ss = _ffn(h, hb, ss, norm_g[i, 2], ffn_w_gate, ffn_w_up, ffn_w_down, (i, 1))
    return _rmsnorm(h, final_norm_g, x.dtype).reshape(batch, seq, d)
```

```python
import functools
import math

import jax
import jax.numpy as jnp
from jax import lax
from jax.experimental import pallas as pl
from jax.experimental.pallas import tpu as pltpu

F32 = jnp.float32
BF16 = jnp.bfloat16
I32 = jnp.int32

HEAD_DIM = 128
ROT_DIM = HEAD_DIM // 4
ROT_HALF = ROT_DIM // 2
ROPE_THETA = 500000.0
RMS_EPS = 1e-6
SUBLN_EPS = 1e-5
MOBA_BLOCK = 256
MOBA_TOPK = 3
DSA_TOPK_MAX = 256
N_MIXERS = 3

LANES = 128
MIB = 1024 * 1024
VMEM_CAP_BYTES = 60 * MIB
NEG_INF = float("-inf")
INT_MIN = -(2 ** 31)


def _cparams(n_axes, vmem_bytes):
    return pltpu.CompilerParams(
        dimension_semantics=("arbitrary",) * n_axes,
        vmem_limit_bytes=int(min(VMEM_CAP_BYTES, vmem_bytes)))


def _pick(n, candidates):
    for c in candidates:
        if n % c == 0:
            return c
    raise ValueError(f"no tile in {candidates} divides {n}")


def _rmsnorm_kernel(x_ref, g_ref, o_ref, *, eps):
    x = x_ref[...]
    y = x * lax.rsqrt(jnp.mean(x * x, axis=-1, keepdims=True) + eps) * g_ref[...]
    o_ref[...] = y.astype(o_ref.dtype)


def _rmsnorm(x, g, out_dtype):
    m, d = x.shape
    tm = _pick(m, (256, 128, 8))
    vmem = 2 * tm * d * (4 + jnp.dtype(out_dtype).itemsize) + 4 * tm * d * 4 + 4 * MIB
    return pl.pallas_call(
        functools.partial(_rmsnorm_kernel, eps=RMS_EPS),
        out_shape=jax.ShapeDtypeStruct((m, d), out_dtype),
        grid=(m // tm,),
        in_specs=[pl.BlockSpec((tm, d), lambda i: (i, 0)),
                  pl.BlockSpec((1, d), lambda i: (0, 0))],
        out_specs=pl.BlockSpec((tm, d), lambda i: (i, 0)),
        compiler_params=_cparams(1, vmem),
        name="rmsnorm",
    )(x, g.reshape(1, d))


def _prep_kernel(x_ref, xb_ref):
    xb_ref[...] = x_ref[...].astype(BF16)


def _prep(x):
    m, d = x.shape
    tm = _pick(m, (256, 128, 8))
    return pl.pallas_call(
        _prep_kernel,
        out_shape=jax.ShapeDtypeStruct((m, d), BF16),
        grid=(m // tm,),
        in_specs=[pl.BlockSpec((tm, d), lambda i: (i, 0))],
        out_specs=pl.BlockSpec((tm, d), lambda i: (i, 0)),
        compiler_params=_cparams(1, 2 * tm * d * 6 + 4 * tm * d * 4 + 4 * MIB),
        name="prep",
    )(x)


def _row_scale(x, n_lanes):
    xf = x.astype(F32)
    ms = jnp.sum(xf * xf, axis=-1, keepdims=True) / x.shape[1]
    return jnp.broadcast_to(lax.rsqrt(ms + RMS_EPS), (x.shape[0], n_lanes))


def _lane_tile(x, n):
    return x if n == 1 else jnp.concatenate([x] * n, axis=1)


def _rope_table_kernel(pos_ref, inv_ref, c_ref, s1_ref, s2_ref):
    ang = pos_ref[...].astype(F32) * inv_ref[...]
    lane = lax.broadcasted_iota(I32, ang.shape, 1)
    cos = jnp.cos(ang)
    sin = jnp.sin(ang)
    c_ref[...] = jnp.where(lane < ROT_DIM, cos, 1.0)
    s1_ref[...] = jnp.where(lane < ROT_HALF, 0.0, jnp.where(lane < ROT_DIM, sin, 0.0))
    s2_ref[...] = jnp.where(lane < ROT_HALF, -sin, 0.0)


def _rope_tables(positions):
    m = positions.size
    inv = ROPE_THETA ** (-jnp.arange(0, ROT_DIM, 2, dtype=F32) / ROT_DIM)
    inv = jnp.concatenate([inv, inv, jnp.zeros((LANES - ROT_DIM,), F32)]).reshape(1, LANES)
    tm = _pick(m, (512, 256, 128, 8))
    spec = pl.BlockSpec((tm, LANES), lambda i: (i, 0))
    return pl.pallas_call(
        _rope_table_kernel,
        out_shape=[jax.ShapeDtypeStruct((m, LANES), F32)] * 3,
        grid=(m // tm,),
        in_specs=[pl.BlockSpec((tm, 1), lambda i: (i, 0)),
                  pl.BlockSpec((1, LANES), lambda i: (0, 0))],
        out_specs=[spec, spec, spec],
        compiler_params=_cparams(1, 16 * MIB),
        name="rope_tables",
    )(positions.reshape(m, 1).astype(I32), inv)


def _rope(a, c, s1, s2):
    return (a * c + pltpu.roll(a, ROT_HALF, 1) * s1
            + pltpu.roll(a, LANES - ROT_HALF, 1) * s2)


def _mm_vmem(tm, tn, k, mt, out_bytes, n_w=1, extra=0):
    return (2 * tm * k * 2 + n_w * (2 * k * tn * 2 + (k // mt) * tn * 4) + 2 * tm * tn * out_bytes
            + (n_w + 1) * tm * tn * 4 + extra + 8 * MIB)


class _WeightStream:
    def __init__(self, w_hbm, widx, stage_ref, wb_ref, sem, g_ref):
        self.w_hbm, self.widx, self.stage, self.wb, self.sem = w_hbm, tuple(widx), stage_ref, wb_ref, sem
        self.g = g_ref
        self.kc, self.tn = stage_ref.shape
        self.n = pl.program_id(0)
        self.m = pl.program_id(1)
        self.have_next = self.n + 1 < pl.num_programs(0)

    def _copy(self, tile, chunk):
        rows = pl.ds(pl.multiple_of(chunk * self.kc, self.kc), self.kc)
        cols = pl.ds(pl.multiple_of(tile * self.tn, self.tn), self.tn)
        return pltpu.make_async_copy(self.w_hbm.at[self.widx + (rows, cols)], self.stage, self.sem)

    def _cast(self, tile, chunk):
        rows = pl.ds(pl.multiple_of(chunk * self.kc, self.kc), self.kc)
        self.wb[tile % 2, rows, :] = (self.stage[...] * self.g[rows, :]).astype(BF16)

    def begin(self):
        @pl.when((self.n == 0) & (self.m == 0))
        def _():
            def body(c, carry):
                cp = self._copy(0, c)
                cp.start()
                cp.wait()
                self._cast(0, c)
                return carry
            lax.fori_loop(0, pl.num_programs(1), body, 0)

        @pl.when(self.have_next)
        def _():
            self._copy(self.n + 1, self.m).start()

    def tile(self):
        return self.wb[self.n % 2]

    def finish(self):
        @pl.when(self.have_next)
        def _():
            self._copy(self.n + 1, self.m).wait()
            self._cast(self.n + 1, self.m)


def _stream_scratch(k, tn, mt):
    assert k % mt == 0 and (k // mt) % 16 == 0, (k, mt)
    return [pltpu.VMEM((k // mt, tn), F32), pltpu.VMEM((2, k, tn), BF16), pltpu.SemaphoreType.DMA(())]


_ANY_SPEC = pl.BlockSpec(memory_space=pl.ANY)


def _gain_spec(k):
    return pl.BlockSpec((k, 1), lambda *_: (0, 0))


def _gain_vmem(k):
    return 2 * k * LANES * 4


class _SideCast:
    def __init__(self, src_hbm, widx, dst_hbm, cin_ref, cout_ref, sem_in, sem_out):
        self.src, self.widx, self.dst = src_hbm, tuple(widx), dst_hbm
        self.cin, self.cout, self.sem_in, self.sem_out = cin_ref, cout_ref, sem_in, sem_out
        self.rps = cin_ref.shape[0]
        self.n_active = dst_hbm.shape[0] // self.rps
        self.n_steps = pl.num_programs(0) * pl.num_programs(1)
        self.s = pl.program_id(0) * pl.num_programs(1) + pl.program_id(1)

    def _rows(self, c):
        return pl.ds(pl.multiple_of(c * self.rps, self.rps), self.rps)

    def _in(self, c):
        return pltpu.make_async_copy(self.src.at[self.widx + (self._rows(c),)], self.cin, self.sem_in)

    def _out(self, c):
        return pltpu.make_async_copy(self.cout, self.dst.at[self._rows(c)], self.sem_out)

    def begin(self):
        @pl.when(self.s < self.n_active)
        def _():
            self._in(self.s).start()

    def finish(self):
        s = self.s

        @pl.when(s < self.n_active)
        def _():
            self._in(s).wait()

            @pl.when(s > 0)
            def _():
                self._out(s - 1).wait()

            self.cout[...] = self.cin[...].astype(BF16)
            self._out(s).start()

        @pl.when((s == self.n_active) | ((s == self.n_steps - 1) & (s == self.n_active - 1)))
        def _():
            self._out(self.n_active - 1).wait()


def _side_cast_rows(rows, n_steps):
    for rps in range(16, rows + 1, 16):
        if rows % rps == 0 and rows // rps <= n_steps:
            return rps
    raise ValueError((rows, n_steps))


def _side_cast_scratch(rows, cols, n_steps):
    rps = _side_cast_rows(rows, n_steps)
    return [pltpu.VMEM((rps, cols), F32), pltpu.VMEM((rps, cols), BF16),
            pltpu.SemaphoreType.DMA(()), pltpu.SemaphoreType.DMA(())]


def _mm_rope_kernel(x_ref, g_ref, w_hbm, c_ref, s1_ref, s2_ref, nxt_hbm, o_ref, nxtb_hbm,
                    stage_ref, wb_ref, sem, cin_ref, cout_ref, sem_in, sem_out,
                    *, widx, tn, n_q_tiles, n_rope_tiles, q_scale):
    j = pl.program_id(0)
    ws = _WeightStream(w_hbm, widx, stage_ref, wb_ref, sem, g_ref)
    side = _SideCast(nxt_hbm, widx, nxtb_hbm, cin_ref, cout_ref, sem_in, sem_out)
    ws.begin()
    side.begin()

    def matmul():
        x = x_ref[...]
        return jnp.dot(x, ws.tile(), preferred_element_type=F32) * _row_scale(x, tn)

    def roped(scale):
        acc = matmul()
        c, s1, s2 = c_ref[...], s1_ref[...], s2_ref[...]
        for g in range(tn // LANES):
            sl = slice(g * LANES, (g + 1) * LANES)
            y = _rope(acc[:, sl], c, s1, s2)
            o_ref[:, sl] = (y if scale is None else y * scale).astype(o_ref.dtype)

    @pl.when(j < n_q_tiles)
    def _():
        roped(q_scale)

    @pl.when((j >= n_q_tiles) & (j < n_rope_tiles))
    def _():
        roped(None)

    @pl.when(j >= n_rope_tiles)
    def _():
        o_ref[...] = matmul().astype(o_ref.dtype)

    ws.finish()
    side.finish()


def _matmul_rope(x, g, w, w_next, widx, tables, n_q_cols, n_rope_cols, q_scale, out_dtype=BF16):
    m, k = x.shape
    n = w.shape[-1]
    tm = _pick(m, (1024, 512, 256, 128))
    tn = _pick(n, (1024, 512, 256, 128))
    mt = m // tm
    assert n_rope_cols % tn == 0 and n_q_cols % tn == 0 and n_q_cols <= n_rope_cols
    tab_spec = pl.BlockSpec((tm, LANES), lambda j, i: (i, 0))
    kern = functools.partial(_mm_rope_kernel, widx=widx, tn=tn, n_q_tiles=n_q_cols // tn,
                             n_rope_tiles=n_rope_cols // tn, q_scale=q_scale)
    nrows, ncols = w_next.shape[-2:]
    n_steps = (n // tn) * mt
    return pl.pallas_call(
        kern,
        out_shape=[jax.ShapeDtypeStruct((m, n), out_dtype), jax.ShapeDtypeStruct((nrows, ncols), BF16)],
        grid=(n // tn, mt),
        in_specs=[pl.BlockSpec((tm, k), lambda j, i: (i, 0)), _gain_spec(k), _ANY_SPEC,
                  tab_spec, tab_spec, tab_spec, _ANY_SPEC],
        out_specs=[pl.BlockSpec((tm, tn), lambda j, i: (i, j)), _ANY_SPEC],
        scratch_shapes=_stream_scratch(k, tn, mt) + _side_cast_scratch(nrows, ncols, n_steps),
        compiler_params=_cparams(2, _mm_vmem(tm, tn, k, mt, 2, extra=8 * tm * LANES * 4 + 2 * tm * tn * 4
                                             + _gain_vmem(k))),
        name="matmul_rope",
    )(x, g.reshape(k, 1), w, *tables, w_next)


def _index_kw_proj(x, g, w_k, w_head, tables, head_scale):
    m, k = x.shape
    nh = w_head.shape[1]
    assert w_k.shape[1] == LANES and nh <= LANES
    w = jnp.concatenate([w_k, w_head, jnp.zeros((k, LANES - nh), w_k.dtype)], axis=1)
    tm = _pick(m, (1024, 512, 256, 128))
    tab_spec = pl.BlockSpec((tm, LANES), lambda i: (i, 0))
    out_spec = pl.BlockSpec((tm, LANES), lambda i: (i, 0))

    def kern(x_ref, g_ref, w_ref, c_ref, s1_ref, s2_ref, ki_ref, wt_ref, wb_ref):
        @pl.when(pl.program_id(0) == 0)
        def _():
            wb_ref[...] = (w_ref[...] * g_ref[...]).astype(BF16)
        x = x_ref[...]
        acc = jnp.dot(x, wb_ref[...], preferred_element_type=F32) * _row_scale(x, 2 * LANES)
        ki_ref[...] = _rope(acc[:, :LANES], c_ref[...], s1_ref[...], s2_ref[...]).astype(BF16)
        wt_ref[...] = acc[:, LANES:] * head_scale

    return pl.pallas_call(
        kern,
        out_shape=[jax.ShapeDtypeStruct((m, LANES), BF16), jax.ShapeDtypeStruct((m, LANES), F32)],
        grid=(m // tm,),
        in_specs=[pl.BlockSpec((tm, k), lambda i: (i, 0)), _gain_spec(k),
                  pl.BlockSpec((k, 2 * LANES), lambda i: (0, 0)),
                  tab_spec, tab_spec, tab_spec],
        out_specs=[out_spec, out_spec],
        scratch_shapes=[pltpu.VMEM((k, 2 * LANES), BF16)],
        compiler_params=_cparams(1, 2 * tm * k * 2 + 2 * k * 2 * LANES * 4 + k * 2 * LANES * 2
                                 + 18 * tm * LANES * 4 + _gain_vmem(k) + 4 * MIB),
        name="index_kw_proj",
    )(x, g.reshape(k, 1), w, *tables)


def _index_q_proj(x, g, w, widx, tables):
    m, k = x.shape
    n = w.shape[-1]
    nh = n // LANES
    tm = _pick(m, (1024, 512, 256, 128))
    tn = _pick(n, (1024, 512, 256, 128))
    mt = m // tm
    hpt = tn // LANES
    tab_spec = pl.BlockSpec((tm, LANES), lambda j, i: (i, 0))

    def kern(x_ref, g_ref, w_hbm, c_ref, s1_ref, s2_ref, o_ref, stage_ref, wb_ref, sem):
        ws = _WeightStream(w_hbm, widx, stage_ref, wb_ref, sem, g_ref)
        ws.begin()
        x = x_ref[...]
        acc = jnp.dot(x, ws.tile(), preferred_element_type=F32) * _row_scale(x, tn)
        c, s1, s2 = c_ref[...], s1_ref[...], s2_ref[...]
        for g in range(hpt):
            o_ref[g] = _rope(acc[:, g * LANES:(g + 1) * LANES], c, s1, s2).astype(BF16)
        ws.finish()

    return pl.pallas_call(
        kern,
        out_shape=jax.ShapeDtypeStruct((nh, m, LANES), BF16),
        grid=(n // tn, mt),
        in_specs=[pl.BlockSpec((tm, k), lambda j, i: (i, 0)), _gain_spec(k), _ANY_SPEC,
                  tab_spec, tab_spec, tab_spec],
        out_specs=pl.BlockSpec((hpt, tm, LANES), lambda j, i: (j, i, 0)),
        scratch_shapes=_stream_scratch(k, tn, mt),
        compiler_params=_cparams(2, _mm_vmem(tm, tn, k, mt, 2, extra=8 * tm * LANES * 4 + 2 * tm * tn * 4
                                             + _gain_vmem(k))),
        name="index_q_proj",
    )(x, g.reshape(k, 1), w, *tables)


def _swiglu_kernel(x_ref, g_ref, wg_hbm, wu_hbm, wd_hbm, o_ref, wdb_hbm,
                   gstage_ref, gb_ref, gsem, ustage_ref, ub_ref, usem, cin_ref, cout_ref, sem_in, sem_out,
                   *, widx):
    wgs = _WeightStream(wg_hbm, widx, gstage_ref, gb_ref, gsem, g_ref)
    wus = _WeightStream(wu_hbm, widx, ustage_ref, ub_ref, usem, g_ref)
    side = _SideCast(wd_hbm, widx, wdb_hbm, cin_ref, cout_ref, sem_in, sem_out)
    wgs.begin()
    wus.begin()
    side.begin()
    x = x_ref[...]
    r = _row_scale(x, o_ref.shape[1])
    g = jnp.dot(x, wgs.tile(), preferred_element_type=F32) * r
    u = jnp.dot(x, wus.tile(), preferred_element_type=F32) * r
    o_ref[...] = (g * jax.nn.sigmoid(g) * u).astype(o_ref.dtype)
    wgs.finish()
    wus.finish()
    side.finish()


def _swiglu_up(x, g, wg, wu, wd, widx):
    m, k = x.shape
    n = wg.shape[-1]
    tm = _pick(m, (1024, 512, 256, 128))
    tn = _pick(n, (512, 256, 128))
    mt = m // tm
    nrows, ncols = wd.shape[-2:]
    n_steps = (n // tn) * mt
    return pl.pallas_call(
        functools.partial(_swiglu_kernel, widx=widx),
        out_shape=[jax.ShapeDtypeStruct((m, n), BF16), jax.ShapeDtypeStruct((nrows, ncols), BF16)],
        grid=(n // tn, mt),
        in_specs=[pl.BlockSpec((tm, k), lambda j, i: (i, 0)), _gain_spec(k),
                  _ANY_SPEC, _ANY_SPEC, _ANY_SPEC],
        out_specs=[pl.BlockSpec((tm, tn), lambda j, i: (i, j)), _ANY_SPEC],
        scratch_shapes=(_stream_scratch(k, tn, mt) + _stream_scratch(k, tn, mt)
                        + _side_cast_scratch(nrows, ncols, n_steps)),
        compiler_params=_cparams(2, _mm_vmem(tm, tn, k, mt, 2, n_w=2,
                                             extra=2 * tm * LANES * 4 + _gain_vmem(k))),
        name="swiglu_up",
    )(x, g.reshape(k, 1), wg, wu, wd)


def _mm_residual_kernel(x_ref, w_ref, r_ref, o_ref, ob_ref, *, scale):
    acc = jnp.dot(x_ref[...], w_ref[...], preferred_element_type=F32)
    if scale != 1.0:
        acc = acc * scale
    o = r_ref[...] + acc
    o_ref[...] = o
    ob_ref[...] = o.astype(BF16)


def _matmul_residual(x, w, res, scale):
    m, k = x.shape
    n = w.shape[1]
    tm = _pick(m, (1024, 512, 256, 128))
    tn = _pick(n, (512, 256, 128))
    vmem = (2 * tm * k * 2 + 2 * k * tn * 2 + 4 * tm * tn * 4 + 2 * tm * tn * 2 + 3 * tm * tn * 4
            + 2 * tm * LANES * 4 + 8 * MIB)
    tile = pl.BlockSpec((tm, tn), lambda i, j: (i, j))
    return pl.pallas_call(
        functools.partial(_mm_residual_kernel, scale=scale),
        out_shape=[jax.ShapeDtypeStruct((m, n), F32), jax.ShapeDtypeStruct((m, n), BF16)],
        grid=(m // tm, n // tn),
        in_specs=[pl.BlockSpec((tm, k), lambda i, j: (i, 0)),
                  pl.BlockSpec((k, tn), lambda i, j: (0, j)),
                  tile],
        out_specs=[tile, tile],
        compiler_params=_cparams(2, vmem),
        name="matmul_residual",
    )(x, w, res)


ROW_BLOCK = 128
LOG2E = math.log2(math.e)
Q_PRESCALE = HEAD_DIM ** -0.5 * LOG2E


def _flash_init(m_ref, l_ref, acc_ref):
    m_ref[...] = jnp.full(m_ref.shape, NEG_INF, F32)
    l_ref[...] = jnp.zeros(l_ref.shape, F32)
    acc_ref[...] = jnp.zeros(acc_ref.shape, F32)


def _qk(q, k):
    return lax.dot_general(q, k, (((1,), (1,)), ((), ())), preferred_element_type=F32)


def _softmax_update(s, m_ref):
    m_prev = m_ref[...]
    m_new = jnp.maximum(m_prev, jnp.max(s, axis=-1, keepdims=True))
    m_safe = jnp.where(m_new == NEG_INF, 0.0, m_new)
    alpha = jnp.exp2(m_prev - m_safe)
    p = jnp.exp2(s - _lane_tile(m_safe, s.shape[1] // LANES))
    m_ref[...] = m_new
    return p, alpha


def _attend(q, k, v1, bias, m_ref, l_ref, acc_ref):
    s = _qk(q, k)
    if bias is not None:
        s = s + bias
    p, alpha = _softmax_update(s, m_ref)
    pv = jnp.dot(p.astype(BF16), v1, preferred_element_type=F32)
    dv = acc_ref.shape[-1]
    l_ref[...] = alpha * l_ref[...] + pv[:, dv:]
    acc_ref[...] = alpha * acc_ref[...] + pv[:, :dv]


def _row_blocks(t):
    r = min(ROW_BLOCK, t)
    return [(rb, slice(rb * r, (rb + 1) * r), r) for rb in range(t // r)]


def _heads_per_step(n_heads, preferred):
    for c in (preferred, 2, 1):
        if c <= preferred and n_heads % c == 0:
            return c
    return 1


def _diff_attn_kernel(lam_ref, g_ref, q_ref, k_ref, v_ref, o_ref, m_ref, l_ref, acc_ref,
                      *, t, hp, lam_init):
    i = pl.program_id(2)
    dh = 2 * HEAD_DIM
    _flash_init(m_ref, l_ref, acc_ref)

    def step(off, width, rows, r, g, bias):
        v = v_ref[pl.ds(off, width), g * dh:(g + 1) * dh]
        ps, alphas = [], []
        for c in range(2):
            cols = slice(g * dh + c * HEAD_DIM, g * dh + (c + 1) * HEAD_DIM)
            s = _qk(q_ref[rows, cols], k_ref[pl.ds(off, width), cols])
            if bias is not None:
                s = s + bias
            slot = 2 * g + c
            p, alpha = _softmax_update(s, m_ref.at[slot, rows])
            l_ref[slot, rows] = alpha * l_ref[slot, rows] + jnp.sum(p, axis=-1, keepdims=True)
            ps.append(p.astype(BF16))
            alphas.append(alpha)
        pv = jnp.dot(jnp.concatenate(ps, axis=0), v, preferred_element_type=F32)
        for c in range(2):
            slot = 2 * g + c
            acc_ref[slot, rows] = (_lane_tile(alphas[c], dh // LANES) * acc_ref[slot, rows]
                                   + pv[c * r:(c + 1) * r])

    diag_off = pl.multiple_of(i * t, t)
    for rb, rows, r in _row_blocks(t):
        width = (rb + 1) * r
        row = rb * r + lax.broadcasted_iota(I32, (r, width), 0)
        col = lax.broadcasted_iota(I32, (r, width), 1)
        bias = jnp.where(col <= row, 0.0, NEG_INF)
        for g in range(hp):
            step(diag_off, width, rows, r, g, bias)

    def body(j, carry):
        off = pl.multiple_of(j * t, t)
        for rb, rows, r in _row_blocks(t):
            for g in range(hp):
                step(off, t, rows, r, g, None)
        return carry

    lax.fori_loop(0, i, body, 0)

    p = lam_ref[...]
    lam = (jnp.exp(jnp.sum(p[0:1] * p[1:2], axis=-1, keepdims=True))
           - jnp.exp(jnp.sum(p[2:3] * p[3:4], axis=-1, keepdims=True)) + lam_init)
    for g in range(hp):
        w1 = _lane_tile(1.0 / l_ref[2 * g], dh // LANES)
        w2 = _lane_tile(lam / l_ref[2 * g + 1], dh // LANES)
        o = acc_ref[2 * g] * w1 - acc_ref[2 * g + 1] * w2
        y = o * lax.rsqrt(jnp.mean(o * o, axis=-1, keepdims=True) + SUBLN_EPS) * g_ref[...]
        o_ref[:, g * dh:(g + 1) * dh] = (y * (1.0 - lam_init)).astype(o_ref.dtype)


def _diff_attention(qkv, lam_params, subln_g, lam_init, batch, seq):
    m, three_d = qkv.shape
    d = three_d // 3
    dh = 2 * HEAD_DIM
    nh = d // dh
    hp = _heads_per_step(nh, 2)
    ng = nh // hp
    w = hp * dh
    t = _pick(seq, (512, 256, 128))
    nq = seq // t
    vmem = (2 * (2 * seq * w * 2) + 4 * t * w * 2 + 2 * hp * t * (2 * LANES + dh) * 4
            + 16 * ROW_BLOCK * t * 4 + 8 * MIB)
    kern = functools.partial(_diff_attn_kernel, t=t, hp=hp, lam_init=lam_init)
    return pl.pallas_call(
        kern,
        out_shape=jax.ShapeDtypeStruct((m, d), BF16),
        grid=(batch, ng, nq),
        in_specs=[pl.BlockSpec((4, HEAD_DIM), lambda b, h, i: (0, 0)),
                  pl.BlockSpec((1, dh), lambda b, h, i: (0, 0)),
                  pl.BlockSpec((t, w), lambda b, h, i: (b * nq + i, h)),
                  pl.BlockSpec((seq, w), lambda b, h, i: (b, ng + h)),
                  pl.BlockSpec((seq, w), lambda b, h, i: (b, 2 * ng + h))],
        out_specs=pl.BlockSpec((t, w), lambda b, h, i: (b * nq + i, h)),
        scratch_shapes=[pltpu.VMEM((2 * hp, t, LANES), F32), pltpu.VMEM((2 * hp, t, LANES), F32),
                        pltpu.VMEM((2 * hp, t, dh), F32)],
        compiler_params=_cparams(3, vmem),
        name="diff_attention",
    )(lam_params, subln_g.reshape(1, dh), qkv, qkv, qkv)


def _moba_attn_kernel(q_ref, k_ref, v_ref, o_ref, kmean_ref, selb_ref, m_ref, l_ref, acc_ref,
                      *, t, hp, blk, nbp, topk):
    i = pl.program_id(2)
    seq = k_ref.shape[0]
    bpc = t // blk
    blk_shift = blk.bit_length() - 1
    heads = [slice(g * HEAD_DIM, (g + 1) * HEAD_DIM) for g in range(hp)]

    @pl.when(i == 0)
    def _():
        r = lax.broadcasted_iota(I32, (nbp, seq), 0)
        c = lax.broadcasted_iota(I32, (nbp, seq), 1)
        avg = jnp.where(c >= r * blk, jnp.where(c < (r + 1) * blk, 1.0 / blk, 0.0), 0.0).astype(BF16)
        for g, sl in enumerate(heads):
            kmean_ref[g] = jnp.dot(avg, k_ref[:, sl], preferred_element_type=F32).astype(BF16)

    n_idx = lax.broadcasted_iota(I32, (nbp, t), 0)
    n_f = n_idx.astype(F32)
    qblk = lax.shift_right_logical(i * t + lax.broadcasted_iota(I32, (nbp, t), 1), blk_shift)
    valid = n_idx < qblk
    pad = jnp.full((LANES - nbp, t), NEG_INF, F32)
    for g, sl in enumerate(heads):
        gate = _qk(kmean_ref[g], q_ref[:, sl])
        gt = jnp.where(valid, gate, NEG_INF)
        picked = jnp.zeros((nbp, t), F32)
        for _ in range(topk):
            mx = jnp.max(gt, axis=0, keepdims=True)
            first = jnp.min(jnp.where(gt == mx, n_f, float(nbp)), axis=0, keepdims=True)
            hit = n_f == first
            picked = jnp.where(hit, 1.0, picked)
            gt = jnp.where(hit, NEG_INF, gt)
        sel_t = jnp.where(valid, jnp.where(picked > 0.5, 0.0, NEG_INF), NEG_INF)
        selb_ref[g] = jnp.concatenate([sel_t, pad], axis=0).T

    _flash_init(m_ref, l_ref, acc_ref)
    ones = jnp.ones((t, LANES), BF16)

    def block_bias(g, rows, r, j, width):
        sb = selb_ref[g, rows, :]
        nr = lax.broadcasted_iota(I32, (r, LANES), 1)
        col_blk = lax.shift_right_logical(lax.broadcasted_iota(I32, (r, width), 1), blk_shift)
        out = None
        for b in range(-(-width // blk)):
            col = jnp.max(jnp.where(nr == j * bpc + b, sb, NEG_INF), axis=-1, keepdims=True)
            out = col if out is None else jnp.where(col_blk >= b, col, out)
        return out

    diag_off = pl.multiple_of(i * t, t)
    for rb, rows, r in _row_blocks(t):
        width = (rb + 1) * r
        row = rb * r + lax.broadcasted_iota(I32, (r, width), 0)
        col = lax.broadcasted_iota(I32, (r, width), 1)
        own = lax.shift_right_logical(row, blk_shift) == lax.shift_right_logical(col, blk_shift)
        for g, sl in enumerate(heads):
            bias = jnp.where(col <= row, jnp.where(own, 0.0, block_bias(g, rows, r, i, width)), NEG_INF)
            v1 = jnp.concatenate([v_ref[pl.ds(diag_off, width), sl], ones[:width]], axis=1)
            _attend(q_ref[rows, sl], k_ref[pl.ds(diag_off, width), sl], v1, bias,
                    m_ref.at[g, rows], l_ref.at[g, rows], acc_ref.at[g, rows])

    def body(j, carry):
        off = pl.multiple_of(j * t, t)
        ks = [k_ref[pl.ds(off, t), sl] for sl in heads]
        v1s = [jnp.concatenate([v_ref[pl.ds(off, t), sl], ones], axis=1) for sl in heads]
        for rb, rows, r in _row_blocks(t):
            for g, sl in enumerate(heads):
                _attend(q_ref[rows, sl], ks[g], v1s[g], block_bias(g, rows, r, j, t),
                        m_ref.at[g, rows], l_ref.at[g, rows], acc_ref.at[g, rows])
        return carry

    lax.fori_loop(0, i, body, 0)
    for g, sl in enumerate(heads):
        o_ref[:, sl] = (acc_ref[g] * (1.0 / l_ref[g])).astype(o_ref.dtype)


def _moba_attention(qkv, batch, seq):
    m, three_d = qkv.shape
    d = three_d // 3
    nh = d // HEAD_DIM
    assert seq % MOBA_BLOCK == 0 and MOBA_BLOCK & (MOBA_BLOCK - 1) == 0
    t = _pick(seq, (512, 256))
    assert t % MOBA_BLOCK == 0 and MOBA_BLOCK % ROW_BLOCK == 0
    hp = _heads_per_step(nh, 4)
    ng = nh // hp
    w = hp * HEAD_DIM
    nq = seq // t
    nb = seq // MOBA_BLOCK
    nbp = -(-nb // 16) * 16
    assert nbp <= LANES
    vmem = (2 * (2 * seq * w * 2) + 4 * t * w * 2 + hp * t * 4 * LANES * 4
            + 16 * ROW_BLOCK * t * 4 + 8 * MIB)
    kern = functools.partial(_moba_attn_kernel, t=t, hp=hp, blk=MOBA_BLOCK, nbp=nbp,
                             topk=min(MOBA_TOPK, nb))
    return pl.pallas_call(
        kern,
        out_shape=jax.ShapeDtypeStruct((m, d), BF16),
        grid=(batch, ng, nq),
        in_specs=[pl.BlockSpec((t, w), lambda b, h, i: (b * nq + i, h)),
                  pl.BlockSpec((seq, w), lambda b, h, i: (b, ng + h)),
                  pl.BlockSpec((seq, w), lambda b, h, i: (b, 2 * ng + h))],
        out_specs=pl.BlockSpec((t, w), lambda b, h, i: (b * nq + i, h)),
        scratch_shapes=[pltpu.VMEM((hp, nbp, HEAD_DIM), BF16), pltpu.VMEM((hp, t, LANES), F32),
                        pltpu.VMEM((hp, t, LANES), F32), pltpu.VMEM((hp, t, LANES), F32),
                        pltpu.VMEM((hp, t, HEAD_DIM), F32)],
        compiler_params=_cparams(3, vmem),
        name="moba_attention",
    )(qkv, qkv, qkv)


def _dsa_index_kernel(qi_ref, ki_ref, w_ref, bias_ref, key_ref, wb_ref, *, tq, tk, nh, topk):
    i = pl.program_id(1)
    w = w_ref[...]
    for h in range(nh):
        wb_ref[h] = jnp.broadcast_to(w[:, h:h + 1], (tq, LANES))

    q_all = qi_ref[...].reshape(nh * tq, HEAD_DIM)
    qpos = i * tq + lax.broadcasted_iota(I32, (tq, tk), 0)
    col = lax.broadcasted_iota(I32, (tq, tk), 1)
    n_chunks = (i * tq + tq + tk - 1) // tk

    def score_chunk(c, carry):
        off = pl.multiple_of(c * tk, tk)
        kc = ki_ref[pl.ds(off, tk), :]
        r = _qk(q_all, kc)
        acc = jnp.zeros((tq, tk), F32)
        for h in range(nh):
            wbh = wb_ref[h]
            wfull = jnp.concatenate([wbh] * (tk // LANES), axis=1)
            acc = acc + jnp.maximum(r[h * tq:(h + 1) * tq], 0.0) * wfull
        bits = lax.bitcast_convert_type(acc + 0.0, I32)
        key = bits ^ (lax.shift_right_arithmetic(bits, 31) & 0x7FFFFFFF)
        key_ref[c] = jnp.where(off + col <= qpos, key, INT_MIN)
        return carry

    lax.fori_loop(0, n_chunks, score_chunk, 0)

    def count_ge(cand):
        def add(c, part):
            ge = jnp.where(key_ref[c] >= cand, 1.0, 0.0)
            return part + sum(ge[:, g * LANES:(g + 1) * LANES] for g in range(tk // LANES))
        part = lax.fori_loop(0, n_chunks, add, jnp.zeros((tq, LANES), F32))
        return jnp.sum(part, axis=-1, keepdims=True)

    zero = jnp.zeros((tq, 1), I32)
    thr = jnp.where(count_ge(zero) >= topk, zero, INT_MIN)

    def bisect(b, thr):
        cand = thr + lax.shift_left(jnp.int32(1), 30 - b)
        return jnp.where(count_ge(cand) >= topk, cand, thr)

    thr = lax.fori_loop(0, 31, bisect, thr)
    thr = jnp.maximum(thr, INT_MIN + 1)
    has_ties = jnp.max(count_ge(thr)) > topk

    bias_ref[...] = jnp.full(bias_ref.shape, NEG_INF, bias_ref.dtype)

    @pl.when(jnp.logical_not(has_ties))
    def _():
        def emit(c, carry):
            bias_ref[c] = jnp.where(key_ref[c] >= thr, 0.0, NEG_INF).astype(bias_ref.dtype)
            return carry
        lax.fori_loop(0, n_chunks, emit, 0)

    @pl.when(has_ties)
    def _():
        need = topk - count_ge(thr + 1)
        upper = jnp.where(lax.broadcasted_iota(I32, (tk, tk), 0) <= lax.broadcasted_iota(I32, (tk, tk), 1),
                          1.0, 0.0).astype(BF16)

        def emit(c, seen):
            key = key_ref[c]
            tied = key == thr
            rank = seen + jnp.dot(jnp.where(tied, 1.0, 0.0).astype(BF16), upper,
                                  preferred_element_type=F32)
            keep = jnp.where(tied, jnp.where(rank <= need, 0.0, NEG_INF), NEG_INF)
            bias_ref[c] = jnp.where(key > thr, 0.0, keep).astype(bias_ref.dtype)
            return rank[:, tk - 1:tk]

        lax.fori_loop(0, n_chunks, emit, jnp.zeros((tq, 1), F32))


def _dsa_index_bias(qi, ki, wts, batch, seq, tk):
    nh, m, _ = qi.shape
    tq = _pick(seq, (128,))
    nq = seq // tq
    nkc = seq // tk
    topk = min(DSA_TOPK_MAX, seq // 4)
    vmem = (2 * nh * tq * HEAD_DIM * 2 + 2 * seq * HEAD_DIM * 2 + 2 * tq * LANES * 4
            + 2 * tq * seq * 2 + tq * seq * 4 + nh * tq * LANES * 4
            + 2 * nh * tq * tk * 4 + 4 * tq * seq * 4 + 6 * MIB)
    kern = functools.partial(_dsa_index_kernel, tq=tq, tk=tk, nh=nh, topk=topk)
    return pl.pallas_call(
        kern,
        out_shape=jax.ShapeDtypeStruct((batch, nkc, seq, tk), BF16),
        grid=(batch, nq),
        in_specs=[pl.BlockSpec((nh, tq, HEAD_DIM), lambda b, i: (0, b * nq + i, 0)),
                  pl.BlockSpec((seq, HEAD_DIM), lambda b, i: (b, 0)),
                  pl.BlockSpec((tq, LANES), lambda b, i: (b * nq + i, 0))],
        out_specs=pl.BlockSpec((None, nkc, tq, tk), lambda b, i: (b, 0, i, 0)),
        scratch_shapes=[pltpu.VMEM((nkc, tq, tk), I32), pltpu.VMEM((nh, tq, LANES), F32)],
        compiler_params=_cparams(2, vmem),
        name="dsa_index_bias",
    )(qi, ki, wts)


def _dsa_attn_kernel(q_ref, k_ref, v_ref, bias_ref, o_ref, m_ref, l_ref, acc_ref, *, t, hp):
    i = pl.program_id(1)
    _flash_init(m_ref, l_ref, acc_ref)
    ones = jnp.ones((t, LANES), BF16)
    heads = [slice(g * HEAD_DIM, (g + 1) * HEAD_DIM) for g in range(hp)]

    diag_off = pl.multiple_of(i * t, t)
    for rb, rows, r in _row_blocks(t):
        width = (rb + 1) * r
        bias = bias_ref[i, rows, :width].astype(F32)
        for g, sl in enumerate(heads):
            v1 = jnp.concatenate([v_ref[pl.ds(diag_off, width), sl], ones[:width]], axis=1)
            _attend(q_ref[rows, sl], k_ref[pl.ds(diag_off, width), sl], v1, bias,
                    m_ref.at[g, rows], l_ref.at[g, rows], acc_ref.at[g, rows])

    def body(j, carry):
        off = pl.multiple_of(j * t, t)
        ks = [k_ref[pl.ds(off, t), sl] for sl in heads]
        v1s = [jnp.concatenate([v_ref[pl.ds(off, t), sl], ones], axis=1) for sl in heads]
        for rb, rows, r in _row_blocks(t):
            bias = bias_ref[j, rows, :].astype(F32)
            for g, sl in enumerate(heads):
                _attend(q_ref[rows, sl], ks[g], v1s[g], bias,
                        m_ref.at[g, rows], l_ref.at[g, rows], acc_ref.at[g, rows])
        return carry

    lax.fori_loop(0, i, body, 0)
    for g, sl in enumerate(heads):
        o_ref[:, sl] = (acc_ref[g] * (1.0 / l_ref[g])).astype(o_ref.dtype)


def _dsa_attention(qkv, bias, batch, seq, t):
    m, three_d = qkv.shape
    d = three_d // 3
    nh = d // HEAD_DIM
    hp = _heads_per_step(nh, 4)
    ng = nh // hp
    w = hp * HEAD_DIM
    nq = seq // t
    nkc = seq // t
    vmem = (2 * (2 * seq * w * 2) + 4 * t * w * 2 + 2 * nkc * t * t * 2
            + hp * t * 3 * LANES * 4 + 16 * ROW_BLOCK * t * 4 + 8 * MIB)
    kern = functools.partial(_dsa_attn_kernel, t=t, hp=hp)
    return pl.pallas_call(
        kern,
        out_shape=jax.ShapeDtypeStruct((m, d), BF16),
        grid=(batch, nq, ng),
        in_specs=[pl.BlockSpec((t, w), lambda b, i, h: (b * nq + i, h)),
                  pl.BlockSpec((seq, w), lambda b, i, h: (b, ng + h)),
                  pl.BlockSpec((seq, w), lambda b, i, h: (b, 2 * ng + h)),
                  pl.BlockSpec((None, nkc, t, t), lambda b, i, h: (b, 0, i, 0))],
        out_specs=pl.BlockSpec((t, w), lambda b, i, h: (b * nq + i, h)),
        scratch_shapes=[pltpu.VMEM((hp, t, LANES), F32), pltpu.VMEM((hp, t, LANES), F32),
                        pltpu.VMEM((hp, t, HEAD_DIM), F32)],
        compiler_params=_cparams(3, vmem),
        name="dsa_attention",
    )(qkv, qkv, qkv, bias)


def _lambda_init_for(layer):
    return 0.8 - 0.6 * math.exp(-0.3 * layer)


def _ffn(h, hb, g, w_gate, w_up, w_down, widx):
    act, w_down_bf16 = _swiglu_up(hb, g, w_gate, w_up, w_down, widx)
    return _matmul_residual(act, w_down_bf16, h, 0.5)


def kernel(x, positions, norm_g, ffn_w_gate, ffn_w_up, ffn_w_down, attn_w_in, attn_w_out,
           diff_lambda_q1, diff_lambda_k1, diff_lambda_q2, diff_lambda_k2, diff_subln_g,
           idx_w_q, idx_w_k, idx_w_head, final_norm_g):
    batch, seq, d = x.shape
    depth = norm_g.shape[0]
    m = batch * seq
    tables = _rope_tables(positions)
    h = x.reshape(m, d)
    hb = _prep(h)
    for i in range(depth):
        h, hb = _ffn(h, hb, norm_g[i, 0], ffn_w_gate, ffn_w_up, ffn_w_down, (i, 0))
        qkv, w_out_bf16 = _matmul_rope(hb, norm_g[i, 1], attn_w_in, attn_w_out, (i,), tables,
                                       d, 2 * d, Q_PRESCALE)
        mixer = i % N_MIXERS
        j = i // N_MIXERS
        if mixer == 0:
            lam_params = jnp.stack([diff_lambda_q1[j], diff_lambda_k1[j],
                                    diff_lambda_q2[j], diff_lambda_k2[j]])
            mix = _diff_attention(qkv, lam_params, diff_subln_g[j], _lambda_init_for(i), batch, seq)
        elif mixer == 1:
            mix = _moba_attention(qkv, batch, seq)
        else:
            n_idx_heads = idx_w_head.shape[-1]
            idx_dim = idx_w_k.shape[-1]
            qi = _index_q_proj(hb, norm_g[i, 1], idx_w_q, (j,), tables)
            ki, wts = _index_kw_proj(hb, norm_g[i, 1], idx_w_k[j], idx_w_head[j], tables,
                                     n_idx_heads ** -0.5 * idx_dim ** -0.5)
            t = _pick(seq, (512, 256))
            bias = _dsa_index_bias(qi, ki, wts, batch, seq, t)
            mix = _dsa_attention(qkv, bias, batch, seq, t)
        h, hb = _matmul_residual(mix, w_out_bf16, h, 1.0)
        h, hb = _ffn(h, hb, norm_g[i, 2], ffn_w_gate, ffn_w_up, ffn_w_down, (i, 1))
    return _rmsnorm(h, final_norm_g, x.dtype).reshape(batch, seq, d)
```
